```python
import math
import jax, jax.numpy as jnp
from jax import lax
import numpy as np

D_MODEL = 1024
BATCH = 16
SEQ = 2048
DEPTH = 2

EPS = 1e-6
POOL_WINDOWS = (2, 4, 8, 16)
POOL_GROUPS = 4
POOL_DH = D_MODEL // 8
POOL_WIDTH = POOL_GROUPS * POOL_DH
CONV_WIDTH = D_MODEL // 2
CONV_K = 3
AB_IN = POOL_WIDTH + 3 * CONV_WIDTH
AB_OUT = POOL_WIDTH + CONV_WIDTH
SGU_CHUNK = 128
SGU_GROUPS = 4
SGU_DH = D_MODEL // 8
SGU_WIDTH = SGU_GROUPS * SGU_DH
SB_HEADS = 8
SB_DH = 64
SB_WIDTH = SB_HEADS * SB_DH
SB_BLOCK = 128
CD_IN = 2 * SGU_WIDTH + 3 * SB_WIDTH
CD_OUT = SGU_WIDTH + SB_WIDTH
D_FF = 4 * D_MODEL
N_AB = (DEPTH + 1) // 2
N_CD = DEPTH // 2

kernel_name = 'hybrid_pool_conv_sgu_stickbreak_trunk'


def rmsnorm(x, g):
    xf = x.astype(jnp.float32)
    y = xf * lax.rsqrt(jnp.mean(xf * xf, axis=-1, keepdims=True) + EPS)
    return (y * g.astype(jnp.float32)).astype(x.dtype)


def layernorm(x, g, b):
    xf = x.astype(jnp.float32)
    mu = jnp.mean(xf, axis=-1, keepdims=True)
    xc = xf - mu
    y = xc * lax.rsqrt(jnp.mean(xc * xc, axis=-1, keepdims=True) + EPS)
    return (y * g.astype(jnp.float32) + b.astype(jnp.float32)).astype(x.dtype)


def pool_mixer(a, w, scale):
    T = a.shape[1]
    af = a.astype(jnp.float32)
    cs = jnp.pad(jnp.cumsum(af, axis=1), ((0, 0), (1, 0), (0, 0), (0, 0)))
    win = jnp.array(POOL_WINDOWS, dtype=jnp.int32)
    pos = jnp.arange(T, dtype=jnp.int32)
    start = jnp.maximum(pos[:, None] + 1 - win[None, :], 0)
    count = jnp.minimum(pos[:, None] + 1, win[None, :]).astype(jnp.float32)
    grp = jnp.arange(POOL_GROUPS, dtype=jnp.int32)
    window_sum = cs[:, 1:] - cs[:, start, grp[None, :], :]
    pooled = (window_sum / count[None, :, :, None] - af).astype(a.dtype)
    mixed = jnp.einsum('btgc,gcd->btgd', pooled, w)
    return mixed * scale


def short_conv(h, w, b):
    T = h.shape[1]
    hp = jnp.pad(h, ((0, 0), (CONV_K - 1, 0), (0, 0)))
    y = hp[:, 0:T] * w[0]
    for k in range(1, CONV_K):
        y = y + hp[:, k:k + T] * w[k]
    return y + b


def spatial_gating(u, v, g, beta, w_s, b_s):
    B, T, _ = v.shape
    v = layernorm(v, g, beta)
    vc = v.reshape(B, T // SGU_CHUNK, SGU_CHUNK, SGU_GROUPS, SGU_DH)
    causal = jnp.tril(jnp.ones((SGU_CHUNK, SGU_CHUNK), dtype=bool))
    w = jnp.where(causal[None], w_s, 0)
    s = jnp.einsum('gts,bnsgc->bntgc', w, vc) + b_s.T[:, :, None]
    return u * s.reshape(B, T, SGU_WIDTH)


def stick_breaking_attention(q, k, v):
    B, T, H, dh = q.shape
    q = q.transpose(0, 2, 1, 3)
    k = k.transpose(0, 2, 1, 3)
    v = v.transpose(0, 2, 1, 3)
    scale = 1.0 / math.sqrt(dh)
    outs = []
    for i in range(T // SB_BLOCK):
        q0 = i * SB_BLOCK
        kend = q0 + SB_BLOCK
        qb = q[:, :, q0:kend]
        kb = k[:, :, :kend]
        vb = v[:, :, :kend]
        z = jnp.einsum('bhqd,bhkd->bhqk', qb, kb,
                       preferred_element_type=jnp.float32) * scale
        qpos = q0 + jnp.arange(SB_BLOCK, dtype=jnp.int32)
        kpos = jnp.arange(kend, dtype=jnp.int32)
        mask = kpos[None, :] < qpos[:, None]
        log_keep = jnp.where(mask, jax.nn.log_sigmoid(-z), 0.0)
        suffix = lax.cumsum(log_keep, axis=3, reverse=True) - log_keep
        weights = jnp.where(mask, jnp.exp(jax.nn.log_sigmoid(z) + suffix), 0.0)
        outs.append(jnp.einsum('bhqk,bhkd->bhqd', weights.astype(vb.dtype), vb))
    o = jnp.concatenate(outs, axis=2)
    return o.transpose(0, 2, 1, 3).reshape(B, T, H * dh)


def setup_inputs(seed: int = 0) -> dict:
    key = jax.random.key(seed)
    ks = jax.random.split(key, 20)
    nrm = jax.random.normal
    f32 = jnp.float32
    res_scale = 1.0 / math.sqrt(2 * DEPTH)
    return {
        'x': nrm(ks[0], (BATCH, SEQ, D_MODEL), f32),
        'mix_norm_g': 1.0 + 0.01 * nrm(ks[1], (DEPTH, D_MODEL), f32),
        'mlp_norm_g': 1.0 + 0.01 * nrm(ks[2], (DEPTH, D_MODEL), f32),
        'ab_w_in': nrm(ks[3], (N_AB, D_MODEL, AB_IN), f32) * D_MODEL ** -0.5,
        'pool_w': nrm(ks[4], (N_AB, POOL_GROUPS, POOL_DH, POOL_DH), f32) * POOL_DH ** -0.5,
        'pool_scale': 1.0 + 0.02 * nrm(ks[5], (N_AB, POOL_GROUPS, POOL_DH), f32),
        'conv_w': nrm(ks[6], (N_AB, CONV_K, CONV_WIDTH), f32) * CONV_K ** -0.5,
        'conv_b': 0.01 * nrm(ks[7], (N_AB, CONV_WIDTH), f32),
        'ab_w_out': nrm(ks[8], (N_AB, AB_OUT, D_MODEL), f32) * AB_OUT ** -0.5 * res_scale,
        'cd_w_in': nrm(ks[9], (N_CD, D_MODEL, CD_IN), f32) * D_MODEL ** -0.5,
        'sgu_norm_g': 1.0 + 0.01 * nrm(ks[10], (N_CD, SGU_WIDTH), f32),
        'sgu_norm_b': 0.01 * nrm(ks[11], (N_CD, SGU_WIDTH), f32),
        'sgu_w': nrm(ks[12], (N_CD, SGU_GROUPS, SGU_CHUNK, SGU_CHUNK), f32) * SGU_CHUNK ** -0.5,
        'sgu_b': 1.0 + 0.01 * nrm(ks[13], (N_CD, SGU_GROUPS, SGU_CHUNK), f32),
        'cd_w_out': nrm(ks[14], (N_CD, CD_OUT, D_MODEL), f32) * CD_OUT ** -0.5 * res_scale,
        'mlp_w1': nrm(ks[15], (DEPTH, D_MODEL, D_FF), f32) * D_MODEL ** -0.5,
        'mlp_w2': nrm(ks[16], (DEPTH, D_FF, D_MODEL), f32) * D_FF ** -0.5 * res_scale,
        'final_norm_g': 1.0 + 0.01 * nrm(ks[17], (D_MODEL,), f32),
    }


def reference(x, mix_norm_g, mlp_norm_g, ab_w_in, pool_w, pool_scale, conv_w, conv_b,
              ab_w_out, cd_w_in, sgu_norm_g, sgu_norm_b, sgu_w, sgu_b, cd_w_out,
              mlp_w1, mlp_w2, final_norm_g):
    B, T, _ = x.shape
    h = x
    for layer in range(DEPTH):
        xn = rmsnorm(h, mix_norm_g[layer])
        if layer % 2 == 0:
            i = layer // 2
            p = xn @ ab_w_in[i]
            a, xb, gate_b, gate_c = jnp.split(
                p, [POOL_WIDTH, POOL_WIDTH + CONV_WIDTH, POOL_WIDTH + 2 * CONV_WIDTH], axis=-1)
            a_out = pool_mixer(a.reshape(B, T, POOL_GROUPS, POOL_DH),
                               pool_w[i], pool_scale[i]).reshape(B, T, POOL_WIDTH)
            b_out = gate_b * short_conv(gate_c * xb, conv_w[i], conv_b[i])
            mix = jnp.concatenate([a_out, b_out], axis=-1) @ ab_w_out[i]
        else:
            i = layer // 2
            p = xn @ cd_w_in[i]
            uv = jax.nn.gelu(p[..., :2 * SGU_WIDTH], approximate=False)
            u, v = jnp.split(uv, 2, axis=-1)
            c_out = spatial_gating(u, v, sgu_norm_g[i], sgu_norm_b[i], sgu_w[i], sgu_b[i])
            qkv = p[..., 2 * SGU_WIDTH:].reshape(B, T, 3, SB_HEADS, SB_DH)
            d_out = stick_breaking_attention(qkv[:, :, 0], qkv[:, :, 1], qkv[:, :, 2])
            mix = jnp.concatenate([c_out, d_out], axis=-1) @ cd_w_out[i]
        h = h + mix
        hn = rmsnorm(h, mlp_norm_g[layer])
        h = h + jnp.square(jax.nn.relu(hn @ mlp_w1[layer])) @ mlp_w2[layer]
    return rmsnorm(h, final_norm_g)
```

```python
import functools
import math

import jax
import jax.numpy as jnp
from jax import lax
from jax.experimental import pallas as pl
from jax.experimental.pallas import tpu as pltpu

F32 = jnp.float32
BF16 = jnp.bfloat16

EPS = 1e-6
POOL_WINDOWS = (2, 4, 8, 16)
GROUP_WIDTH = 128
MIXER_WIDTH = 512
CONV_K = 3
SGU_CHUNK = 128
SB_DH = 64

V7X_LANES = 128
V7X_SUBLANES = 8
V7X_VMEM_BYTES = 64 * 1024 * 1024
VMEM_LIMIT_BYTES = V7X_VMEM_BYTES - 8 * 1024 * 1024

ROW_TILE = 512
POOL_HALO = 16
CONV_HALO = V7X_SUBLANES
ATT_TQ = 128
ATT_TK = 256
MLP_FF_CHUNK = 1024


def _resident(shape):
    zeros = (0,) * len(shape)
    return pl.BlockSpec(shape, lambda *_: zeros, pipeline_mode=pl.Buffered(1))


def _rmsnorm(x, g):
    ms = jnp.mean(x * x, axis=-1, keepdims=True)
    return x * lax.rsqrt(ms + EPS) * g


def _trailing_window_sum(e, window):
    s = e
    span = 1
    while span < window:
        s = s + pltpu.roll(s, span, axis=0)
        span *= 2
    return s


def _l0_mixer_kernel(x_ref, g_ref, win_ref, poolw_ref, pscale_ref, convw_ref, convb_ref,
                     wout_ref, o_ref, a_carry, u_carry):
    j = pl.program_id(1)
    tm = x_ref.shape[0]

    @pl.when(j == 0)
    def _():
        a_carry[...] = jnp.zeros_like(a_carry)
        u_carry[...] = jnp.zeros_like(u_carry)

    x = x_ref[...]
    xn = _rmsnorm(x, g_ref[...]).astype(BF16)
    p = jnp.dot(xn, win_ref[...], preferred_element_type=F32)
    w = MIXER_WIDTH
    a, xb, gate_b, gate_c = p[:, :w], p[:, w:2 * w], p[:, 2 * w:3 * w], p[:, 3 * w:]

    a_ext = jnp.concatenate([a_carry[...], a], axis=0)
    a_carry[...] = a[tm - POOL_HALO:, :]
    pos = j * tm + lax.broadcasted_iota(jnp.int32, (tm, 1), 0)
    outs = []
    for grp, window in enumerate(POOL_WINDOWS):
        cols = slice(grp * GROUP_WIDTH, (grp + 1) * GROUP_WIDTH)
        wsum = _trailing_window_sum(a_ext[:, cols], window)[POOL_HALO:, :]
        count = jnp.minimum(pos + 1, window).astype(F32)
        pooled = wsum / count - a[:, cols]
        mixed = jnp.dot(pooled.astype(BF16), poolw_ref[grp], preferred_element_type=F32)
        outs.append(mixed * pscale_ref[grp:grp + 1, :])

    u = gate_c * xb
    u_ext = jnp.concatenate([u_carry[...], u], axis=0)
    u_carry[...] = u[tm - CONV_HALO:, :]
    u1 = pltpu.roll(u_ext, 1, axis=0)[CONV_HALO:, :]
    u2 = pltpu.roll(u_ext, 2, axis=0)[CONV_HALO:, :]
    y = u2 * convw_ref[0:1, :] + u1 * convw_ref[1:2, :] + u * convw_ref[2:3, :] + convb_ref[...]
    outs.append(gate_b * y)

    mix_in = jnp.concatenate(outs, axis=1).astype(BF16)
    o_ref[...] = x + jnp.dot(mix_in, wout_ref[...], preferred_element_type=F32)


def _l0_mixer(x, g, w_in, pool_w, pool_scale, conv_w, conv_b, w_out):
    B, T, D = x.shape
    tm = min(ROW_TILE, T)
    grid = (B, T // tm)
    row_spec = pl.BlockSpec((None, tm, D), lambda b, j: (b, j, 0))
    return pl.pallas_call(
        _l0_mixer_kernel,
        out_shape=jax.ShapeDtypeStruct((B, T, D), F32),
        grid=grid,
        in_specs=[row_spec, _resident(g.shape), _resident(w_in.shape), _resident(pool_w.shape),
                  _resident(pool_scale.shape), _resident(conv_w.shape), _resident(conv_b.shape),
                  _resident(w_out.shape)],
        out_specs=row_spec,
        scratch_shapes=[pltpu.VMEM((POOL_HALO, MIXER_WIDTH), F32),
                        pltpu.VMEM((CONV_HALO, MIXER_WIDTH), F32)],
        compiler_params=pltpu.CompilerParams(
            dimension_semantics=("arbitrary", "arbitrary"), vmem_limit_bytes=VMEM_LIMIT_BYTES),
        name="l0_mixer",
    )(x, g, w_in, pool_w, pool_scale, conv_w, conv_b, w_out)


def _mlp_body(h, g_ref, w1_ref, w2_ref):
    d_ff = w1_ref.shape[1]
    chunk = min(MLP_FF_CHUNK, d_ff)
    hn = _rmsnorm(h, g_ref[...]).astype(BF16)
    acc = h
    for c in range(d_ff // chunk):
        cols = slice(c * chunk, (c + 1) * chunk)
        a = jnp.dot(hn, w1_ref[:, cols], preferred_element_type=F32)
        r = jnp.maximum(a, 0.0)
        acc = acc + jnp.dot((r * r).astype(BF16), w2_ref[cols, :], preferred_element_type=F32)
    return acc


def _mlp_kernel(h_ref, g_ref, w1_ref, w2_ref, o_ref):
    o_ref[...] = _mlp_body(h_ref[...], g_ref, w1_ref, w2_ref)


def _mix_mlp_final_kernel(h_ref, c_ref, d_ref, wout_ref, g_ref, w1_ref, w2_ref, fg_ref, o_ref):
    w = MIXER_WIDTH
    h = h_ref[...]
    h = h + (jnp.dot(c_ref[...], wout_ref[:w, :], preferred_element_type=F32)
             + jnp.dot(d_ref[...], wout_ref[w:, :], preferred_element_type=F32))
    o_ref[...] = _rmsnorm(_mlp_body(h, g_ref, w1_ref, w2_ref), fg_ref[...])


def _mlp(h, g, w1, w2):
    N, D = h.shape
    tm = min(ROW_TILE, N)
    row_spec = pl.BlockSpec((tm, D), lambda i: (i, 0))
    return pl.pallas_call(
        _mlp_kernel,
        out_shape=jax.ShapeDtypeStruct((N, D), F32),
        grid=(N // tm,),
        in_specs=[row_spec, _resident(g.shape), _resident(w1.shape), _resident(w2.shape)],
        out_specs=row_spec,
        compiler_params=pltpu.CompilerParams(
            dimension_semantics=("arbitrary",), vmem_limit_bytes=VMEM_LIMIT_BYTES),
        name="mlp0",
    )(h, g, w1, w2)


def _mix_mlp_final(h, c, d, w_out, g, w1, w2, final_g):
    N, D = h.shape
    tm = min(ROW_TILE, N)
    row_spec = pl.BlockSpec((tm, D), lambda i: (i, 0))
    half_spec = pl.BlockSpec((tm, MIXER_WIDTH), lambda i: (i, 0))
    return pl.pallas_call(
        _mix_mlp_final_kernel,
        out_shape=jax.ShapeDtypeStruct((N, D), F32),
        grid=(N // tm,),
        in_specs=[row_spec, half_spec, half_spec, _resident(w_out.shape), _resident(g.shape),
                  _resident(w1.shape), _resident(w2.shape), _resident(final_g.shape)],
        out_specs=row_spec,
        compiler_params=pltpu.CompilerParams(
            dimension_semantics=("arbitrary",), vmem_limit_bytes=VMEM_LIMIT_BYTES),
        name="l1_out_mlp1_final",
    )(h, c, d, w_out, g, w1, w2, final_g)


def _gelu(x):
    return 0.5 * x * (1.0 + lax.erf(x * (1.0 / math.sqrt(2.0))))


def _l1_front_kernel(h_ref, g_ref, win_ref, lng_ref, lnb_ref, ws_ref, bs_ref,
                     c_ref, q_ref, k_ref, v_ref):
    tm = h_ref.shape[0]
    w = MIXER_WIDTH
    xn = _rmsnorm(h_ref[...], g_ref[...]).astype(BF16)
    p = jnp.dot(xn, win_ref[...], preferred_element_type=F32)
    u = _gelu(p[:, :w])
    v = _gelu(p[:, w:2 * w])
    q_ref[...] = (p[:, 2 * w:3 * w] * (1.0 / math.sqrt(SB_DH))).astype(BF16)
    k_ref[...] = p[:, 3 * w:4 * w].astype(BF16)
    v_ref[...] = p[:, 4 * w:].astype(BF16)

    mu = jnp.mean(v, axis=-1, keepdims=True)
    vc = v - mu
    var = jnp.mean(vc * vc, axis=-1, keepdims=True)
    vn = (vc * lax.rsqrt(var + EPS) * lng_ref[...] + lnb_ref[...]).astype(BF16)

    L = SGU_CHUNK
    tpos = lax.broadcasted_iota(jnp.int32, (L, L), 0)
    spos = lax.broadcasted_iota(jnp.int32, (L, L), 1)
    causal = spos <= tpos
    for grp in range(w // GROUP_WIDTH):
        cols = slice(grp * GROUP_WIDTH, (grp + 1) * GROUP_WIDTH)
        w_s = jnp.where(causal, ws_ref[grp], 0.0).astype(BF16)
        bias = bs_ref[grp]
        for n in range(tm // L):
            rows = slice(n * L, (n + 1) * L)
            s = jnp.dot(w_s, vn[rows, cols], preferred_element_type=F32) + bias
            c_ref[rows, cols] = (u[rows, cols] * s).astype(BF16)


def _l1_front(h, g, w_in, ln_g, ln_b, w_s, b_s):
    N, D = h.shape
    tm = min(ROW_TILE, N)
    row_spec = pl.BlockSpec((tm, D), lambda i: (i, 0))
    half_spec = pl.BlockSpec((tm, MIXER_WIDTH), lambda i: (i, 0))
    half_shape = jax.ShapeDtypeStruct((N, MIXER_WIDTH), BF16)
    return pl.pallas_call(
        _l1_front_kernel,
        out_shape=(half_shape,) * 4,
        grid=(N // tm,),
        in_specs=[row_spec, _resident(g.shape), _resident(w_in.shape), _resident(ln_g.shape),
                  _resident(ln_b.shape), _resident(w_s.shape), _resident(b_s.shape)],
        out_specs=(half_spec,) * 4,
        compiler_params=pltpu.CompilerParams(
            dimension_semantics=("arbitrary",), vmem_limit_bytes=VMEM_LIMIT_BYTES),
        name="l1_front",
    )(h, g, w_in, ln_g, ln_b, w_s, b_s)


def _attn_kernel(q_ref, k_ref, v_ref, upper_ref, o_ref, acc_ref, carry_ref):
    T = q_ref.shape[0]
    tq, tk = ATT_TQ, ATT_TK
    lane = lax.broadcasted_iota(jnp.int32, (tq, V7X_LANES), 1)
    first_head = lane < SB_DH
    row_q = lax.broadcasted_iota(jnp.int32, (2 * tq, 1), 0) & (tq - 1)
    col_k = lax.broadcasted_iota(jnp.int32, (1, tk), 1)

    def process(qs, k0, mask):
        kc = k_ref[pl.ds(k0, tk), :]
        vc = v_ref[pl.ds(k0, tk), :]
        z = lax.dot_general(qs, kc, (((1,), (1,)), ((), ())), preferred_element_type=F32)
        softplus = jnp.maximum(z, 0.0) + jnp.log(1.0 + jnp.exp(-jnp.abs(z)))
        log_keep = -softplus
        log_beta = z - softplus
        if mask is not None:
            log_keep = jnp.where(mask, log_keep, 0.0)
        hi = log_keep.astype(BF16)
        lo = (log_keep - hi.astype(F32)).astype(BF16)
        upper = upper_ref[...]
        suffix = (jnp.dot(hi, upper, preferred_element_type=F32)
                  + jnp.dot(lo, upper, preferred_element_type=F32))
        carry = carry_ref[...]
        carry_wide = jnp.concatenate([carry] * (tk // V7X_LANES), axis=1)
        wgt = jnp.exp(log_beta + suffix + carry_wide)
        if mask is not None:
            wgt = jnp.where(mask, wgt, 0.0)
        acc_ref[...] += jnp.dot(wgt.astype(BF16), vc, preferred_element_type=F32)
        carry_ref[...] = carry + jnp.sum(log_keep, axis=1, keepdims=True)

    def q_block(i, _):
        q0 = pl.multiple_of(i * tq, tq)
        q2 = q_ref[pl.ds(q0, tq), :]
        zero = jnp.zeros_like(q2)
        qs = jnp.concatenate([jnp.where(first_head, q2, zero), jnp.where(first_head, zero, q2)],
                             axis=0)
        acc_ref[...] = jnp.zeros_like(acc_ref)
        carry_ref[...] = jnp.zeros_like(carry_ref)
        diag = i // (tk // tq)
        d0 = pl.multiple_of(diag * tk, tk)
        process(qs, d0, (d0 + col_k) < (q0 + row_q))

        def below(t, _):
            process(qs, pl.multiple_of((diag - 1 - t) * tk, tk), None)
            return 0

        lax.fori_loop(0, diag, below, 0)
        acc = acc_ref[...]
        o_ref[pl.ds(q0, tq), :] = jnp.where(first_head, acc[:tq], acc[tq:]).astype(o_ref.dtype)
        return 0

    lax.fori_loop(0, T // tq, q_block, 0)


def _attention(q, k, v):
    B, T, W = q.shape
    upper = jnp.triu(jnp.ones((ATT_TK, ATT_TK), BF16), 1).T
    seq_spec = pl.BlockSpec((None, T, V7X_LANES), lambda b, hp: (b, 0, hp))
    return pl.pallas_call(
        _attn_kernel,
        out_shape=jax.ShapeDtypeStruct((B, T, W), BF16),
        grid=(B, W // V7X_LANES),
        in_specs=[seq_spec, seq_spec, seq_spec, _resident(upper.shape)],
        out_specs=seq_spec,
        scratch_shapes=[pltpu.VMEM((2 * ATT_TQ, V7X_LANES), F32),
                        pltpu.VMEM((2 * ATT_TQ, V7X_LANES), F32)],
        compiler_params=pltpu.CompilerParams(
            dimension_semantics=("arbitrary", "arbitrary"), vmem_limit_bytes=VMEM_LIMIT_BYTES),
        name="stick_breaking_attention",
    )(q, k, v, upper)


def kernel(x, mix_norm_g, mlp_norm_g, ab_w_in, pool_w, pool_scale, conv_w, conv_b, ab_w_out,
           cd_w_in, sgu_norm_g, sgu_norm_b, sgu_w, sgu_b, cd_w_out, mlp_w1, mlp_w2, final_norm_g):
    B, T, D = x.shape
    N = B * T
    row = lambda vec: vec.reshape(1, -1)

    h = _l0_mixer(x, row(mix_norm_g[0]), ab_w_in[0].astype(BF16), pool_w[0].astype(BF16),
                  pool_scale[0], conv_w[0], row(conv_b[0]), ab_w_out[0].astype(BF16))
    h = _mlp(h.reshape(N, D), row(mlp_norm_g[0]), mlp_w1[0].astype(BF16), mlp_w2[0].astype(BF16))

    b_s = jnp.broadcast_to(sgu_b[0][:, :, None], sgu_b[0].shape + (GROUP_WIDTH,))
    c, q, k, v = _l1_front(h, row(mix_norm_g[1]), cd_w_in[0].astype(BF16), row(sgu_norm_g[0]),
                           row(sgu_norm_b[0]), sgu_w[0], b_s)
    shape3 = (B, T, MIXER_WIDTH)
    d = _attention(q.reshape(shape3), k.reshape(shape3), v.reshape(shape3)).reshape(N, MIXER_WIDTH)
    out = _mix_mlp_final(h, c, d, cd_w_out[0].astype(BF16), row(mlp_norm_g[1]),
                         mlp_w1[1].astype(BF16), mlp_w2[1].astype(BF16), row(final_norm_g))
    return out.reshape(B, T, D)
```

```python
import functools
import math

import jax
import jax.numpy as jnp
from jax import lax
from jax.experimental import pallas as pl
from jax.experimental.pallas import tpu as pltpu

F32 = jnp.float32
BF16 = jnp.bfloat16

EPS = 1e-6
POOL_WINDOWS = (2, 4, 8, 16)
GROUP_WIDTH = 128
MIXER_WIDTH = 512
CONV_K = 3
SGU_CHUNK = 128
SB_DH = 64

V7X_LANES = 128
V7X_SUBLANES = 8
V7X_VMEM_BYTES = 64 * 1024 * 1024
VMEM_LIMIT_BYTES = V7X_VMEM_BYTES - 8 * 1024 * 1024

ROW_TILE = 512
POOL_HALO = 16
CONV_HALO = V7X_SUBLANES
ATT_TQ = 256
ATT_TK = 256
MLP_FF_CHUNK = 1024


def _resident(shape):
    zeros = (0,) * len(shape)
    return pl.BlockSpec(shape, lambda *_: zeros, pipeline_mode=pl.Buffered(1))


def _rmsnorm(x, g):
    ms = jnp.mean(x * x, axis=-1, keepdims=True)
    return x * lax.rsqrt(ms + EPS) * g


def _trailing_window_sum(e, window):
    s = e
    span = 1
    while span < window:
        s = s + pltpu.roll(s, span, axis=0)
        span *= 2
    return s


def _l0_mixer_kernel(x_ref, g_ref, win_ref, poolw_ref, pscale_ref, convw_ref, convb_ref,
                     wout_ref, o_ref, a_carry, u_carry):
    j = pl.program_id(1)
    tm = x_ref.shape[0]

    @pl.when(j == 0)
    def _():
        a_carry[...] = jnp.zeros_like(a_carry)
        u_carry[...] = jnp.zeros_like(u_carry)

    x = x_ref[...]
    xn = _rmsnorm(x, g_ref[...]).astype(BF16)
    p = jnp.dot(xn, win_ref[...], preferred_element_type=F32)
    w = MIXER_WIDTH
    a, xb, gate_b, gate_c = p[:, :w], p[:, w:2 * w], p[:, 2 * w:3 * w], p[:, 3 * w:]

    a_ext = jnp.concatenate([a_carry[...], a], axis=0)
    a_carry[...] = a[tm - POOL_HALO:, :]
    pos = j * tm + lax.broadcasted_iota(jnp.int32, (tm, 1), 0)
    outs = []
    for grp, window in enumerate(POOL_WINDOWS):
        cols = slice(grp * GROUP_WIDTH, (grp + 1) * GROUP_WIDTH)
        wsum = _trailing_window_sum(a_ext[:, cols], window)[POOL_HALO:, :]
        count = jnp.minimum(pos + 1, window).astype(F32)
        pooled = wsum / count - a[:, cols]
        mixed = jnp.dot(pooled.astype(BF16), poolw_ref[grp], preferred_element_type=F32)
        outs.append(mixed * pscale_ref[grp:grp + 1, :])

    u = gate_c * xb
    u_ext = jnp.concatenate([u_carry[...], u], axis=0)
    u_carry[...] = u[tm - CONV_HALO:, :]
    u1 = pltpu.roll(u_ext, 1, axis=0)[CONV_HALO:, :]
    u2 = pltpu.roll(u_ext, 2, axis=0)[CONV_HALO:, :]
    y = u2 * convw_ref[0:1, :] + u1 * convw_ref[1:2, :] + u * convw_ref[2:3, :] + convb_ref[...]
    outs.append(gate_b * y)

    mix_in = jnp.concatenate(outs, axis=1).astype(BF16)
    o_ref[...] = x + jnp.dot(mix_in, wout_ref[...], preferred_element_type=F32)


def _l0_mixer(x, g, w_in, pool_w, pool_scale, conv_w, conv_b, w_out):
    B, T, D = x.shape
    tm = min(ROW_TILE, T)
    grid = (B, T // tm)
    row_spec = pl.BlockSpec((None, tm, D), lambda b, j: (b, j, 0))
    return pl.pallas_call(
        _l0_mixer_kernel,
        out_shape=jax.ShapeDtypeStruct((B, T, D), F32),
        grid=grid,
        in_specs=[row_spec, _resident(g.shape), _resident(w_in.shape), _resident(pool_w.shape),
                  _resident(pool_scale.shape), _resident(conv_w.shape), _resident(conv_b.shape),
                  _resident(w_out.shape)],
        out_specs=row_spec,
        scratch_shapes=[pltpu.VMEM((POOL_HALO, MIXER_WIDTH), F32),
                        pltpu.VMEM((CONV_HALO, MIXER_WIDTH), F32)],
        compiler_params=pltpu.CompilerParams(
            dimension_semantics=("arbitrary", "arbitrary"), vmem_limit_bytes=VMEM_LIMIT_BYTES),
        name="l0_mixer",
    )(x, g, w_in, pool_w, pool_scale, conv_w, conv_b, w_out)


def _mlp_body(h, g_ref, w1_ref, w2_ref):
    d_ff = w1_ref.shape[1]
    chunk = min(MLP_FF_CHUNK, d_ff)
    hn = _rmsnorm(h, g_ref[...]).astype(BF16)
    acc = h
    for c in range(d_ff // chunk):
        cols = slice(c * chunk, (c + 1) * chunk)
        a = jnp.dot(hn, w1_ref[:, cols], preferred_element_type=F32)
        r = jnp.maximum(a, 0.0)
        acc = acc + jnp.dot((r * r).astype(BF16), w2_ref[cols, :], preferred_element_type=F32)
    return acc


def _mlp_kernel(h_ref, g_ref, w1_ref, w2_ref, o_ref):
    o_ref[...] = _mlp_body(h_ref[...], g_ref, w1_ref, w2_ref)


def _mix_mlp_final_kernel(h_ref, c_ref, d_ref, wout_ref, g_ref, w1_ref, w2_ref, fg_ref, o_ref):
    w = MIXER_WIDTH
    h = h_ref[...]
    h = h + (jnp.dot(c_ref[...], wout_ref[:w, :], preferred_element_type=F32)
             + jnp.dot(d_ref[...], wout_ref[w:, :], preferred_element_type=F32))
    o_ref[...] = _rmsnorm(_mlp_body(h, g_ref, w1_ref, w2_ref), fg_ref[...])


def _mlp(h, g, w1, w2):
    N, D = h.shape
    tm = min(ROW_TILE, N)
    row_spec = pl.BlockSpec((tm, D), lambda i: (i, 0))
    return pl.pallas_call(
        _mlp_kernel,
        out_shape=jax.ShapeDtypeStruct((N, D), F32),
        grid=(N // tm,),
        in_specs=[row_spec, _resident(g.shape), _resident(w1.shape), _resident(w2.shape)],
        out_specs=row_spec,
        compiler_params=pltpu.CompilerParams(
            dimension_semantics=("arbitrary",), vmem_limit_bytes=VMEM_LIMIT_BYTES),
        name="mlp0",
    )(h, g, w1, w2)


def _mix_mlp_final(h, c, d, w_out, g, w1, w2, final_g):
    N, D = h.shape
    tm = min(ROW_TILE, N)
    row_spec = pl.BlockSpec((tm, D), lambda i: (i, 0))
    half_spec = pl.BlockSpec((tm, MIXER_WIDTH), lambda i: (i, 0))
    return pl.pallas_call(
        _mix_mlp_final_kernel,
        out_shape=jax.ShapeDtypeStruct((N, D), F32),
        grid=(N // tm,),
        in_specs=[row_spec, half_spec, half_spec, _resident(w_out.shape), _resident(g.shape),
                  _resident(w1.shape), _resident(w2.shape), _resident(final_g.shape)],
        out_specs=row_spec,
        compiler_params=pltpu.CompilerParams(
            dimension_semantics=("arbitrary",), vmem_limit_bytes=VMEM_LIMIT_BYTES),
        name="l1_out_mlp1_final",
    )(h, c, d, w_out, g, w1, w2, final_g)


def _gelu(x):
    return 0.5 * x * (1.0 + lax.erf(x * (1.0 / math.sqrt(2.0))))


def _l1_front_kernel(h_ref, g_ref, win_ref, lng_ref, lnb_ref, ws_ref, bs_ref,
                     c_ref, q_ref, k_ref, v_ref):
    tm = h_ref.shape[0]
    w = MIXER_WIDTH
    xn = _rmsnorm(h_ref[...], g_ref[...]).astype(BF16)
    p = jnp.dot(xn, win_ref[...], preferred_element_type=F32)
    u = _gelu(p[:, :w])
    v = _gelu(p[:, w:2 * w])
    q_ref[...] = (p[:, 2 * w:3 * w] * (1.0 / math.sqrt(SB_DH))).astype(BF16)
    k_ref[...] = p[:, 3 * w:4 * w].astype(BF16)
    v_ref[...] = p[:, 4 * w:].astype(BF16)

    mu = jnp.mean(v, axis=-1, keepdims=True)
    vc = v - mu
    var = jnp.mean(vc * vc, axis=-1, keepdims=True)
    vn = (vc * lax.rsqrt(var + EPS) * lng_ref[...] + lnb_ref[...]).astype(BF16)

    L = SGU_CHUNK
    tpos = lax.broadcasted_iota(jnp.int32, (L, L), 0)
    spos = lax.broadcasted_iota(jnp.int32, (L, L), 1)
    causal = spos <= tpos
    for grp in range(w // GROUP_WIDTH):
        cols = slice(grp * GROUP_WIDTH, (grp + 1) * GROUP_WIDTH)
        w_s = jnp.where(causal, ws_ref[grp], 0.0).astype(BF16)
        bias = bs_ref[grp]
        for n in range(tm // L):
            rows = slice(n * L, (n + 1) * L)
            s = jnp.dot(w_s, vn[rows, cols], preferred_element_type=F32) + bias
            c_ref[rows, cols] = (u[rows, cols] * s).astype(BF16)


def _l1_front(h, g, w_in, ln_g, ln_b, w_s, b_s):
    N, D = h.shape
    tm = min(ROW_TILE, N)
    row_spec = pl.BlockSpec((tm, D), lambda i: (i, 0))
    half_spec = pl.BlockSpec((tm, MIXER_WIDTH), lambda i: (i, 0))
    half_shape = jax.ShapeDtypeStruct((N, MIXER_WIDTH), BF16)
    return pl.pallas_call(
        _l1_front_kernel,
        out_shape=(half_shape,) * 4,
        grid=(N // tm,),
        in_specs=[row_spec, _resident(g.shape), _resident(w_in.shape), _resident(ln_g.shape),
                  _resident(ln_b.shape), _resident(w_s.shape), _resident(b_s.shape)],
        out_specs=(half_spec,) * 4,
        compiler_params=pltpu.CompilerParams(
            dimension_semantics=("arbitrary",), vmem_limit_bytes=VMEM_LIMIT_BYTES),
        name="l1_front",
    )(h, g, w_in, ln_g, ln_b, w_s, b_s)


LOG2E = math.log2(math.e)


def _attn_kernel(q_ref, k_ref, v_ref, neg_upper_ref, o_ref, acc_ref, carry_ref):
    T, W = q_ref.shape
    n_pairs = W // V7X_LANES
    tq, tk = ATT_TQ, ATT_TK
    lane = lax.broadcasted_iota(jnp.int32, (tq, V7X_LANES), 1)
    first_head = lane < SB_DH
    row_q = lax.broadcasted_iota(jnp.int32, (2 * tq, 1), 0) & (tq - 1)
    col_k = lax.broadcasted_iota(jnp.int32, (1, tk), 1)

    def process(qs, k0, mask):
        pairs = range(n_pairs)
        cols = [slice(hp * V7X_LANES, (hp + 1) * V7X_LANES) for hp in pairs]
        z = [lax.dot_general(qs[hp], k_ref[pl.ds(k0, tk), cols[hp]], (((1,), (1,)), ((), ())),
                             preferred_element_type=F32) for hp in pairs]
        log_beta, keep_cost, row_sum = [], [], []
        for hp in pairs:
            softplus = (jnp.maximum(z[hp], 0.0)
                        + jnp.log(1.0 + jnp.exp2(jnp.abs(z[hp]) * -LOG2E)))
            log_beta.append(z[hp] - softplus)
            if mask is not None:
                softplus = jnp.where(mask, softplus, 0.0)
            keep_cost.append(softplus.astype(BF16))
            row_sum.append(jnp.sum(softplus, axis=1, keepdims=True))
        neg_upper = neg_upper_ref[...]
        suffix = [jnp.dot(keep_cost[hp], neg_upper, preferred_element_type=F32) for hp in pairs]
        wgt = []
        for hp in pairs:
            carry = carry_ref[hp]
            carry_wide = jnp.concatenate([carry] * (tk // V7X_LANES), axis=1)
            w = jnp.exp2((log_beta[hp] + suffix[hp] + carry_wide) * LOG2E)
            if mask is not None:
                w = jnp.where(mask, w, 0.0)
            wgt.append(w.astype(BF16))
            carry_ref[hp] = carry - row_sum[hp]
        for hp in pairs:
            acc_ref[hp] += jnp.dot(wgt[hp], v_ref[pl.ds(k0, tk), cols[hp]],
                                   preferred_element_type=F32)

    def q_block(i, _):
        q0 = pl.multiple_of(i * tq, tq)
        qs = []
        for hp in range(n_pairs):
            q2 = q_ref[pl.ds(q0, tq), hp * V7X_LANES:(hp + 1) * V7X_LANES]
            zero = jnp.zeros_like(q2)
            qs.append(jnp.concatenate(
                [jnp.where(first_head, q2, zero), jnp.where(first_head, zero, q2)], axis=0))
        acc_ref[...] = jnp.zeros_like(acc_ref)
        carry_ref[...] = jnp.zeros_like(carry_ref)
        diag = i // (tk // tq)
        d0 = pl.multiple_of(diag * tk, tk)
        process(qs, d0, (d0 + col_k) < (q0 + row_q))

        def below(t, _):
            process(qs, pl.multiple_of((diag - 1 - t) * tk, tk), None)
            return 0

        lax.fori_loop(0, diag, below, 0)
        for hp in range(n_pairs):
            acc = acc_ref[hp]
            o_ref[pl.ds(q0, tq), hp * V7X_LANES:(hp + 1) * V7X_LANES] = jnp.where(
                first_head, acc[:tq], acc[tq:]).astype(o_ref.dtype)
        return 0

    lax.fori_loop(0, T // tq, q_block, 0)


def _attention(q, k, v):
    B, T, W = q.shape
    neg_upper = -jnp.triu(jnp.ones((ATT_TK, ATT_TK), BF16), 1).T
    seq_spec = pl.BlockSpec((None, T, W), lambda b: (b, 0, 0))
    n_pairs = W // V7X_LANES
    return pl.pallas_call(
        _attn_kernel,
        out_shape=jax.ShapeDtypeStruct((B, T, W), BF16),
        grid=(B,),
        in_specs=[seq_spec, seq_spec, seq_spec, _resident(neg_upper.shape)],
        out_specs=seq_spec,
        scratch_shapes=[pltpu.VMEM((n_pairs, 2 * ATT_TQ, V7X_LANES), F32),
                        pltpu.VMEM((n_pairs, 2 * ATT_TQ, V7X_LANES), F32)],
        compiler_params=pltpu.CompilerParams(
            dimension_semantics=("arbitrary",), vmem_limit_bytes=VMEM_LIMIT_BYTES),
        name="stick_breaking_attention",
    )(q, k, v, neg_upper)


def kernel(x, mix_norm_g, mlp_norm_g, ab_w_in, pool_w, pool_scale, conv_w, conv_b, ab_w_out,
           cd_w_in, sgu_norm_g, sgu_norm_b, sgu_w, sgu_b, cd_w_out, mlp_w1, mlp_w2, final_norm_g):
    B, T, D = x.shape
    N = B * T
    row = lambda vec: vec.reshape(1, -1)

    h = _l0_mixer(x, row(mix_norm_g[0]), ab_w_in[0].astype(BF16), pool_w[0].astype(BF16),
                  pool_scale[0], conv_w[0], row(conv_b[0]), ab_w_out[0].astype(BF16))
    h = _mlp(h.reshape(N, D), row(mlp_norm_g[0]), mlp_w1[0].astype(BF16), mlp_w2[0].astype(BF16))

    b_s = jnp.broadcast_to(sgu_b[0][:, :, None], sgu_b[0].shape + (GROUP_WIDTH,))
    c, q, k, v = _l1_front(h, row(mix_norm_g[1]), cd_w_in[0].astype(BF16), row(sgu_norm_g[0]),
                           row(sgu_norm_b[0]), sgu_w[0], b_s)
    shape3 = (B, T, MIXER_WIDTH)
    d = _attention(q.reshape(shape3), k.reshape(shape3), v.reshape(shape3)).reshape(N, MIXER_WIDTH)
    out = _mix_mlp_final(h, c, d, cd_w_out[0].astype(BF16), row(mlp_norm_g[1]),
                         mlp_w1[1].astype(BF16), mlp_w2[1].astype(BF16), row(final_norm_g))
    return out.reshape(B, T, D)
```

```python
import functools
import math

import jax
import jax.numpy as jnp
from jax import lax
from jax.experimental import pallas as pl
from jax.experimental.pallas import tpu as pltpu

F32 = jnp.float32
BF16 = jnp.bfloat16

EPS = 1e-6
POOL_WINDOWS = (2, 4, 8, 16)
GROUP_WIDTH = 128
MIXER_WIDTH = 512
CONV_K = 3
SGU_CHUNK = 128
SB_DH = 64

V7X_LANES = 128
V7X_SUBLANES = 8
V7X_VMEM_BYTES = 64 * 1024 * 1024
VMEM_LIMIT_BYTES = V7X_VMEM_BYTES - 8 * 1024 * 1024

ROW_TILE = 512
POOL_HALO = 16
CONV_HALO = V7X_SUBLANES
ATT_TQ = 256
ATT_TK = 256
MLP_FF_CHUNK = 1024


def _resident(shape):
    zeros = (0,) * len(shape)
    return pl.BlockSpec(shape, lambda *_: zeros, pipeline_mode=pl.Buffered(1))


def _rmsnorm(x, g):
    ms = jnp.mean(x * x, axis=-1, keepdims=True)
    return x * lax.rsqrt(ms + EPS) * g


def _trailing_window_sum(e, window):
    s = e
    span = 1
    while span < window:
        s = s + pltpu.roll(s, span, axis=0)
        span *= 2
    return s


def _l0_mixer_kernel(x_ref, g_ref, win_ref, poolw_ref, pscale_ref, convw_ref, convb_ref,
                     wout_ref, o_ref, a_carry, u_carry):
    j = pl.program_id(1)
    tm = x_ref.shape[0]

    @pl.when(j == 0)
    def _():
        a_carry[...] = jnp.zeros_like(a_carry)
        u_carry[...] = jnp.zeros_like(u_carry)

    x = x_ref[...]
    xn = _rmsnorm(x, g_ref[...]).astype(BF16)
    p = jnp.dot(xn, win_ref[...], preferred_element_type=F32)
    w = MIXER_WIDTH
    a, xb, gate_b, gate_c = p[:, :w], p[:, w:2 * w], p[:, 2 * w:3 * w], p[:, 3 * w:]

    a_ext = jnp.concatenate([a_carry[...], a], axis=0)
    a_carry[...] = a[tm - POOL_HALO:, :]
    pos = j * tm + lax.broadcasted_iota(jnp.int32, (tm, 1), 0)
    outs = []
    for grp, window in enumerate(POOL_WINDOWS):
        cols = slice(grp * GROUP_WIDTH, (grp + 1) * GROUP_WIDTH)
        wsum = _trailing_window_sum(a_ext[:, cols], window)[POOL_HALO:, :]
        count = jnp.minimum(pos + 1, window).astype(F32)
        pooled = wsum / count - a[:, cols]
        mixed = jnp.dot(pooled.astype(BF16), poolw_ref[grp], preferred_element_type=F32)
        outs.append(mixed * pscale_ref[grp:grp + 1, :])

    u = gate_c * xb
    u_ext = jnp.concatenate([u_carry[...], u], axis=0)
    u_carry[...] = u[tm - CONV_HALO:, :]
    u1 = pltpu.roll(u_ext, 1, axis=0)[CONV_HALO:, :]
    u2 = pltpu.roll(u_ext, 2, axis=0)[CONV_HALO:, :]
    y = u2 * convw_ref[0:1, :] + u1 * convw_ref[1:2, :] + u * convw_ref[2:3, :] + convb_ref[...]
    outs.append(gate_b * y)

    mix_in = jnp.concatenate(outs, axis=1).astype(BF16)
    o_ref[...] = x + jnp.dot(mix_in, wout_ref[...], preferred_element_type=F32)


def _l0_mixer(x, g, w_in, pool_w, pool_scale, conv_w, conv_b, w_out):
    B, T, D = x.shape
    tm = min(ROW_TILE, T)
    grid = (B, T // tm)
    row_spec = pl.BlockSpec((None, tm, D), lambda b, j: (b, j, 0))
    return pl.pallas_call(
        _l0_mixer_kernel,
        out_shape=jax.ShapeDtypeStruct((B, T, D), F32),
        grid=grid,
        in_specs=[row_spec, _resident(g.shape), _resident(w_in.shape), _resident(pool_w.shape),
                  _resident(pool_scale.shape), _resident(conv_w.shape), _resident(conv_b.shape),
                  _resident(w_out.shape)],
        out_specs=row_spec,
        scratch_shapes=[pltpu.VMEM((POOL_HALO, MIXER_WIDTH), F32),
                        pltpu.VMEM((CONV_HALO, MIXER_WIDTH), F32)],
        compiler_params=pltpu.CompilerParams(
            dimension_semantics=("arbitrary", "arbitrary"), vmem_limit_bytes=VMEM_LIMIT_BYTES),
        name="l0_mixer",
    )(x, g, w_in, pool_w, pool_scale, conv_w, conv_b, w_out)


def _mlp_body(h, g_ref, w1_ref, w2_ref):
    d_ff = w1_ref.shape[1]
    chunk = min(MLP_FF_CHUNK, d_ff)
    hn = _rmsnorm(h, g_ref[...]).astype(BF16)
    acc = h
    for c in range(d_ff // chunk):
        cols = slice(c * chunk, (c + 1) * chunk)
        a = jnp.dot(hn, w1_ref[:, cols], preferred_element_type=F32)
        r = jnp.maximum(a, 0.0)
        acc = acc + jnp.dot((r * r).astype(BF16), w2_ref[cols, :], preferred_element_type=F32)
    return acc


def _mlp_kernel(h_ref, g_ref, w1_ref, w2_ref, o_ref):
    o_ref[...] = _mlp_body(h_ref[...], g_ref, w1_ref, w2_ref)


def _mix_mlp_final_kernel(h_ref, c_ref, d_ref, wout_ref, g_ref, w1_ref, w2_ref, fg_ref, o_ref):
    w = MIXER_WIDTH
    h = h_ref[...]
    h = h + (jnp.dot(c_ref[...], wout_ref[:w, :], preferred_element_type=F32)
             + jnp.dot(d_ref[...], wout_ref[w:, :], preferred_element_type=F32))
    o_ref[...] = _rmsnorm(_mlp_body(h, g_ref, w1_ref, w2_ref), fg_ref[...])


def _mlp(h, g, w1, w2):
    N, D = h.shape
    tm = min(ROW_TILE, N)
    row_spec = pl.BlockSpec((tm, D), lambda i: (i, 0))
    return pl.pallas_call(
        _mlp_kernel,
        out_shape=jax.ShapeDtypeStruct((N, D), F32),
        grid=(N // tm,),
        in_specs=[row_spec, _resident(g.shape), _resident(w1.shape), _resident(w2.shape)],
        out_specs=row_spec,
        compiler_params=pltpu.CompilerParams(
            dimension_semantics=("arbitrary",), vmem_limit_bytes=VMEM_LIMIT_BYTES),
        name="mlp0",
    )(h, g, w1, w2)


def _mix_mlp_final(h, c, d, w_out, g, w1, w2, final_g):
    N, D = h.shape
    tm = min(ROW_TILE, N)
    row_spec = pl.BlockSpec((tm, D), lambda i: (i, 0))
    half_spec = pl.BlockSpec((tm, MIXER_WIDTH), lambda i: (i, 0))
    return pl.pallas_call(
        _mix_mlp_final_kernel,
        out_shape=jax.ShapeDtypeStruct((N, D), F32),
        grid=(N // tm,),
        in_specs=[row_spec, half_spec, half_spec, _resident(w_out.shape), _resident(g.shape),
                  _resident(w1.shape), _resident(w2.shape), _resident(final_g.shape)],
        out_specs=row_spec,
        compiler_params=pltpu.CompilerParams(
            dimension_semantics=("arbitrary",), vmem_limit_bytes=VMEM_LIMIT_BYTES),
        name="l1_out_mlp1_final",
    )(h, c, d, w_out, g, w1, w2, final_g)


def _gelu(x):
    return 0.5 * x * (1.0 + lax.erf(x * (1.0 / math.sqrt(2.0))))


def _l1_front_kernel(h_ref, g_ref, win_ref, lng_ref, lnb_ref, ws_ref, bs_ref,
                     c_ref, q_ref, k_ref, v_ref):
    tm = h_ref.shape[0]
    w = MIXER_WIDTH
    xn = _rmsnorm(h_ref[...], g_ref[...]).astype(BF16)
    p = jnp.dot(xn, win_ref[...], preferred_element_type=F32)
    u = _gelu(p[:, :w])
    v = _gelu(p[:, w:2 * w])
    q_ref[...] = (p[:, 2 * w:3 * w] * (1.0 / math.sqrt(SB_DH))).astype(BF16)
    k_ref[...] = p[:, 3 * w:4 * w].astype(BF16)
    v_ref[...] = p[:, 4 * w:].astype(BF16)

    mu = jnp.mean(v, axis=-1, keepdims=True)
    vc = v - mu
    var = jnp.mean(vc * vc, axis=-1, keepdims=True)
    vn = (vc * lax.rsqrt(var + EPS) * lng_ref[...] + lnb_ref[...]).astype(BF16)

    L = SGU_CHUNK
    tpos = lax.broadcasted_iota(jnp.int32, (L, L), 0)
    spos = lax.broadcasted_iota(jnp.int32, (L, L), 1)
    causal = spos <= tpos
    for grp in range(w // GROUP_WIDTH):
        cols = slice(grp * GROUP_WIDTH, (grp + 1) * GROUP_WIDTH)
        w_s = jnp.where(causal, ws_ref[grp], 0.0).astype(BF16)
        bias = bs_ref[grp]
        for n in range(tm // L):
            rows = slice(n * L, (n + 1) * L)
            s = jnp.dot(w_s, vn[rows, cols], preferred_element_type=F32) + bias
            c_ref[rows, cols] = (u[rows, cols] * s).astype(BF16)


def _l1_front(h, g, w_in, ln_g, ln_b, w_s, b_s):
    N, D = h.shape
    tm = min(ROW_TILE, N)
    row_spec = pl.BlockSpec((tm, D), lambda i: (i, 0))
    half_spec = pl.BlockSpec((tm, MIXER_WIDTH), lambda i: (i, 0))
    half_shape = jax.ShapeDtypeStruct((N, MIXER_WIDTH), BF16)
    return pl.pallas_call(
        _l1_front_kernel,
        out_shape=(half_shape,) * 4,
        grid=(N // tm,),
        in_specs=[row_spec, _resident(g.shape), _resident(w_in.shape), _resident(ln_g.shape),
                  _resident(ln_b.shape), _resident(w_s.shape), _resident(b_s.shape)],
        out_specs=(half_spec,) * 4,
        compiler_params=pltpu.CompilerParams(
            dimension_semantics=("arbitrary",), vmem_limit_bytes=VMEM_LIMIT_BYTES),
        name="l1_front",
    )(h, g, w_in, ln_g, ln_b, w_s, b_s)


LOG2E = math.log2(math.e)
WEIGHT_UNDERFLOW_LOG = -104.0


def _attn_kernel(q_ref, k_ref, v_ref, neg_upper_ref, o_ref, acc_ref, carry_ref):
    T, W = q_ref.shape
    n_pairs = W // V7X_LANES
    tq, tk = ATT_TQ, ATT_TK
    lane = lax.broadcasted_iota(jnp.int32, (tq, V7X_LANES), 1)
    first_head = lane < SB_DH
    row_q = lax.broadcasted_iota(jnp.int32, (2 * tq, 1), 0) & (tq - 1)
    col_k = lax.broadcasted_iota(jnp.int32, (1, tk), 1)

    def process(qs, k0, mask):
        pairs = range(n_pairs)
        cols = [slice(hp * V7X_LANES, (hp + 1) * V7X_LANES) for hp in pairs]
        z = [lax.dot_general(qs[hp], k_ref[pl.ds(k0, tk), cols[hp]], (((1,), (1,)), ((), ())),
                             preferred_element_type=F32) for hp in pairs]
        log_beta, keep_cost, row_sum = [], [], []
        for hp in pairs:
            softplus = (jnp.maximum(z[hp], 0.0)
                        + jnp.log(1.0 + jnp.exp2(jnp.abs(z[hp]) * -LOG2E)))
            log_beta.append(z[hp] - softplus)
            if mask is not None:
                softplus = jnp.where(mask, softplus, 0.0)
            keep_cost.append(softplus.astype(BF16))
            row_sum.append(jnp.sum(softplus, axis=1, keepdims=True))
        neg_upper = neg_upper_ref[...]
        suffix = [jnp.dot(keep_cost[hp], neg_upper, preferred_element_type=F32) for hp in pairs]
        wgt = []
        for hp in pairs:
            carry = carry_ref[hp]
            carry_wide = jnp.concatenate([carry] * (tk // V7X_LANES), axis=1)
            w = jnp.exp2((log_beta[hp] + suffix[hp] + carry_wide) * LOG2E)
            if mask is not None:
                w = jnp.where(mask, w, 0.0)
            wgt.append(w.astype(BF16))
            carry_ref[hp] = carry - row_sum[hp]
        for hp in pairs:
            acc_ref[hp] += jnp.dot(wgt[hp], v_ref[pl.ds(k0, tk), cols[hp]],
                                   preferred_element_type=F32)

    def q_block(i, _):
        q0 = pl.multiple_of(i * tq, tq)
        qs = []
        for hp in range(n_pairs):
            q2 = q_ref[pl.ds(q0, tq), hp * V7X_LANES:(hp + 1) * V7X_LANES]
            zero = jnp.zeros_like(q2)
            qs.append(jnp.concatenate(
                [jnp.where(first_head, q2, zero), jnp.where(first_head, zero, q2)], axis=0))
        acc_ref[...] = jnp.zeros_like(acc_ref)
        carry_ref[...] = jnp.zeros_like(carry_ref)
        diag = i // (tk // tq)
        d0 = pl.multiple_of(diag * tk, tk)
        process(qs, d0, (d0 + col_k) < (q0 + row_q))

        def largest_carry():
            top = carry_ref[0]
            for hp in range(1, n_pairs):
                top = jnp.maximum(top, carry_ref[hp])
            return jnp.max(top)

        def more_to_do(state):
            t, top = state
            return jnp.logical_and(t < diag, top > WEIGHT_UNDERFLOW_LOG)

        def below(state):
            t, _ = state
            process(qs, pl.multiple_of((diag - 1 - t) * tk, tk), None)
            return t + 1, largest_carry()

        lax.while_loop(more_to_do, below, (jnp.int32(0), largest_carry()))
        for hp in range(n_pairs):
            acc = acc_ref[hp]
            o_ref[pl.ds(q0, tq), hp * V7X_LANES:(hp + 1) * V7X_LANES] = jnp.where(
                first_head, acc[:tq], acc[tq:]).astype(o_ref.dtype)
        return 0

    lax.fori_loop(0, T // tq, q_block, 0)


def _attention(q, k, v):
    B, T, W = q.shape
    neg_upper = -jnp.triu(jnp.ones((ATT_TK, ATT_TK), BF16), 1).T
    seq_spec = pl.BlockSpec((None, T, W), lambda b: (b, 0, 0))
    n_pairs = W // V7X_LANES
    return pl.pallas_call(
        _attn_kernel,
        out_shape=jax.ShapeDtypeStruct((B, T, W), BF16),
        grid=(B,),
        in_specs=[seq_spec, seq_spec, seq_spec, _resident(neg_upper.shape)],
        out_specs=seq_spec,
        scratch_shapes=[pltpu.VMEM((n_pairs, 2 * ATT_TQ, V7X_LANES), F32),
                        pltpu.VMEM((n_pairs, 2 * ATT_TQ, V7X_LANES), F32)],
        compiler_params=pltpu.CompilerParams(
            dimension_semantics=("arbitrary",), vmem_limit_bytes=VMEM_LIMIT_BYTES),
        name="stick_breaking_attention",
    )(q, k, v, neg_upper)


def kernel(x, mix_norm_g, mlp_norm_g, ab_w_in, pool_w, pool_scale, conv_w, conv_b, ab_w_out,
           cd_w_in, sgu_norm_g, sgu_norm_b, sgu_w, sgu_b, cd_w_out, mlp_w1, mlp_w2, final_norm_g):
    B, T, D = x.shape
    N = B * T
    row = lambda vec: vec.reshape(1, -1)

    h = _l0_mixer(x, row(mix_norm_g[0]), ab_w_in[0].astype(BF16), pool_w[0].astype(BF16),
                  pool_scale[0], conv_w[0], row(conv_b[0]), ab_w_out[0].astype(BF16))
    h = _mlp(h.reshape(N, D), row(mlp_norm_g[0]), mlp_w1[0].astype(BF16), mlp_w2[0].astype(BF16))

    b_s = jnp.broadcast_to(sgu_b[0][:, :, None], sgu_b[0].shape + (GROUP_WIDTH,))
    c, q, k, v = _l1_front(h, row(mix_norm_g[1]), cd_w_in[0].astype(BF16), row(sgu_norm_g[0]),
                           row(sgu_norm_b[0]), sgu_w[0], b_s)
    shape3 = (B, T, MIXER_WIDTH)
    d = _attention(q.reshape(shape3), k.reshape(shape3), v.reshape(shape3)).reshape(N, MIXER_WIDTH)
    out = _mix_mlp_final(h, c, d, cd_w_out[0].astype(BF16), row(mlp_norm_g[1]),
                         mlp_w1[1].astype(BF16), mlp_w2[1].astype(BF16), row(final_norm_g))
    return out.reshape(B, T, D)
```

```python
import functools
import math

import jax
import jax.numpy as jnp
from jax import lax
from jax.experimental import pallas as pl
from jax.experimental.pallas import tpu as pltpu

F32 = jnp.float32
BF16 = jnp.bfloat16

EPS = 1e-6
POOL_WINDOWS = (2, 4, 8, 16)
GROUP_WIDTH = 128
MIXER_WIDTH = 512
CONV_K = 3
SGU_CHUNK = 128
SB_DH = 64

V7X_LANES = 128
V7X_SUBLANES = 8
V7X_VMEM_BYTES = 64 * 1024 * 1024
VMEM_LIMIT_BYTES = V7X_VMEM_BYTES - 8 * 1024 * 1024

ROW_TILE = 512
POOL_HALO = 16
CONV_HALO = V7X_SUBLANES
ATT_TQ = 256
ATT_TK = 256
MLP_FF_CHUNK = 1024


def _resident(shape):
    zeros = (0,) * len(shape)
    return pl.BlockSpec(shape, lambda *_: zeros, pipeline_mode=pl.Buffered(1))


def _rmsnorm(x, g):
    ms = jnp.mean(x * x, axis=-1, keepdims=True)
    return x * lax.rsqrt(ms + EPS) * g


def _trailing_window_sum(e, window):
    s = e
    span = 1
    while span < window:
        s = s + pltpu.roll(s, span, axis=0)
        span *= 2
    return s


def _l0_mixer_kernel(x_ref, g_ref, win_ref, poolw_ref, pscale_ref, convw_ref, convb_ref,
                     wout_ref, o_ref, a_carry, u_carry):
    j = pl.program_id(1)
    tm = x_ref.shape[0]

    @pl.when(j == 0)
    def _():
        a_carry[...] = jnp.zeros_like(a_carry)
        u_carry[...] = jnp.zeros_like(u_carry)

    x = x_ref[...]
    xn = _rmsnorm(x, g_ref[...]).astype(BF16)
    p = jnp.dot(xn, win_ref[...], preferred_element_type=F32)
    w = MIXER_WIDTH
    a, xb, gate_b, gate_c = p[:, :w], p[:, w:2 * w], p[:, 2 * w:3 * w], p[:, 3 * w:]

    a_ext = jnp.concatenate([a_carry[...], a], axis=0)
    a_carry[...] = a[tm - POOL_HALO:, :]
    pos = j * tm + lax.broadcasted_iota(jnp.int32, (tm, 1), 0)
    outs = []
    for grp, window in enumerate(POOL_WINDOWS):
        cols = slice(grp * GROUP_WIDTH, (grp + 1) * GROUP_WIDTH)
        wsum = _trailing_window_sum(a_ext[:, cols], window)[POOL_HALO:, :]
        count = jnp.minimum(pos + 1, window).astype(F32)
        pooled = wsum / count - a[:, cols]
        mixed = jnp.dot(pooled.astype(BF16), poolw_ref[grp], preferred_element_type=F32)
        outs.append(mixed * pscale_ref[grp:grp + 1, :])

    u = gate_c * xb
    u_ext = jnp.concatenate([u_carry[...], u], axis=0)
    u_carry[...] = u[tm - CONV_HALO:, :]
    u1 = pltpu.roll(u_ext, 1, axis=0)[CONV_HALO:, :]
    u2 = pltpu.roll(u_ext, 2, axis=0)[CONV_HALO:, :]
    y = u2 * convw_ref[0:1, :] + u1 * convw_ref[1:2, :] + u * convw_ref[2:3, :] + convb_ref[...]
    outs.append(gate_b * y)

    mix_in = jnp.concatenate(outs, axis=1).astype(BF16)
    o_ref[...] = x + jnp.dot(mix_in, wout_ref[...], preferred_element_type=F32)


def _l0_mixer(x, g, w_in, pool_w, pool_scale, conv_w, conv_b, w_out):
    B, T, D = x.shape
    tm = min(ROW_TILE, T)
    grid = (B, T // tm)
    row_spec = pl.BlockSpec((None, tm, D), lambda b, j: (b, j, 0))
    return pl.pallas_call(
        _l0_mixer_kernel,
        out_shape=jax.ShapeDtypeStruct((B, T, D), F32),
        grid=grid,
        in_specs=[row_spec, _resident(g.shape), _resident(w_in.shape), _resident(pool_w.shape),
                  _resident(pool_scale.shape), _resident(conv_w.shape), _resident(conv_b.shape),
                  _resident(w_out.shape)],
        out_specs=row_spec,
        scratch_shapes=[pltpu.VMEM((POOL_HALO, MIXER_WIDTH), F32),
                        pltpu.VMEM((CONV_HALO, MIXER_WIDTH), F32)],
        compiler_params=pltpu.CompilerParams(
            dimension_semantics=("arbitrary", "arbitrary"), vmem_limit_bytes=VMEM_LIMIT_BYTES),
        name="l0_mixer",
    )(x, g, w_in, pool_w, pool_scale, conv_w, conv_b, w_out)


def _mlp_body(h, g_ref, w1_ref, w2_ref):
    d_ff = w1_ref.shape[1]
    chunk = min(MLP_FF_CHUNK, d_ff)
    hn = _rmsnorm(h, g_ref[...]).astype(BF16)
    acc = h
    for c in range(d_ff // chunk):
        cols = slice(c * chunk, (c + 1) * chunk)
        a = jnp.dot(hn, w1_ref[:, cols], preferred_element_type=F32)
        r = jnp.maximum(a, 0.0)
        acc = acc + jnp.dot((r * r).astype(BF16), w2_ref[cols, :], preferred_element_type=F32)
    return acc


def _mlp_kernel(h_ref, g_ref, w1_ref, w2_ref, o_ref):
    o_ref[...] = _mlp_body(h_ref[...], g_ref, w1_ref, w2_ref)


def _mix_mlp_final_kernel(h_ref, c_ref, d_ref, wout_ref, g_ref, w1_ref, w2_ref, fg_ref, o_ref):
    w = MIXER_WIDTH
    h = h_ref[...]
    h = h + (jnp.dot(c_ref[...], wout_ref[:w, :], preferred_element_type=F32)
             + jnp.dot(d_ref[...], wout_ref[w:, :], preferred_element_type=F32))
    o_ref[...] = _rmsnorm(_mlp_body(h, g_ref, w1_ref, w2_ref), fg_ref[...])


def _mlp(h, g, w1, w2):
    N, D = h.shape
    tm = min(ROW_TILE, N)
    row_spec = pl.BlockSpec((tm, D), lambda i: (i, 0))
    return pl.pallas_call(
        _mlp_kernel,
        out_shape=jax.ShapeDtypeStruct((N, D), F32),
        grid=(N // tm,),
        in_specs=[row_spec, _resident(g.shape), _resident(w1.shape), _resident(w2.shape)],
        out_specs=row_spec,
        compiler_params=pltpu.CompilerParams(
            dimension_semantics=("arbitrary",), vmem_limit_bytes=VMEM_LIMIT_BYTES),
        name="mlp0",
    )(h, g, w1, w2)


def _mix_mlp_final(h, c, d, w_out, g, w1, w2, final_g):
    N, D = h.shape
    tm = min(ROW_TILE, N)
    row_spec = pl.BlockSpec((tm, D), lambda i: (i, 0))
    half_spec = pl.BlockSpec((tm, MIXER_WIDTH), lambda i: (i, 0))
    return pl.pallas_call(
        _mix_mlp_final_kernel,
        out_shape=jax.ShapeDtypeStruct((N, D), F32),
        grid=(N // tm,),
        in_specs=[row_spec, half_spec, half_spec, _resident(w_out.shape), _resident(g.shape),
                  _resident(w1.shape), _resident(w2.shape), _resident(final_g.shape)],
        out_specs=row_spec,
        compiler_params=pltpu.CompilerParams(
            dimension_semantics=("arbitrary",), vmem_limit_bytes=VMEM_LIMIT_BYTES),
        name="l1_out_mlp1_final",
    )(h, c, d, w_out, g, w1, w2, final_g)


def _gelu(x):
    return 0.5 * x * (1.0 + lax.erf(x * (1.0 / math.sqrt(2.0))))


def _l1_front_kernel(h_ref, g_ref, win_ref, lng_ref, lnb_ref, ws_ref, bs_ref,
                     c_ref, q_ref, k_ref, v_ref):
    tm = h_ref.shape[0]
    w = MIXER_WIDTH
    xn = _rmsnorm(h_ref[...], g_ref[...]).astype(BF16)
    p = jnp.dot(xn, win_ref[...], preferred_element_type=F32)
    u = _gelu(p[:, :w])
    v = _gelu(p[:, w:2 * w])
    q_ref[...] = (p[:, 2 * w:3 * w] * (1.0 / math.sqrt(SB_DH))).astype(BF16)
    k_ref[...] = p[:, 3 * w:4 * w].astype(BF16)
    v_ref[...] = p[:, 4 * w:].astype(BF16)

    mu = jnp.mean(v, axis=-1, keepdims=True)
    vc = v - mu
    var = jnp.mean(vc * vc, axis=-1, keepdims=True)
    vn = (vc * lax.rsqrt(var + EPS) * lng_ref[...] + lnb_ref[...]).astype(BF16)

    L = SGU_CHUNK
    tpos = lax.broadcasted_iota(jnp.int32, (L, L), 0)
    spos = lax.broadcasted_iota(jnp.int32, (L, L), 1)
    causal = spos <= tpos
    for grp in range(w // GROUP_WIDTH):
        cols = slice(grp * GROUP_WIDTH, (grp + 1) * GROUP_WIDTH)
        w_s = jnp.where(causal, ws_ref[grp], 0.0).astype(BF16)
        bias = bs_ref[grp]
        for n in range(tm // L):
            rows = slice(n * L, (n + 1) * L)
            s = jnp.dot(w_s, vn[rows, cols], preferred_element_type=F32) + bias
            c_ref[rows, cols] = (u[rows, cols] * s).astype(BF16)


def _l1_front(h, g, w_in, ln_g, ln_b, w_s, b_s):
    N, D = h.shape
    tm = min(ROW_TILE, N)
    row_spec = pl.BlockSpec((tm, D), lambda i: (i, 0))
    half_spec = pl.BlockSpec((tm, MIXER_WIDTH), lambda i: (i, 0))
    half_shape = jax.ShapeDtypeStruct((N, MIXER_WIDTH), BF16)
    return pl.pallas_call(
        _l1_front_kernel,
        out_shape=(half_shape,) * 4,
        grid=(N // tm,),
        in_specs=[row_spec, _resident(g.shape), _resident(w_in.shape), _resident(ln_g.shape),
                  _resident(ln_b.shape), _resident(w_s.shape), _resident(b_s.shape)],
        out_specs=(half_spec,) * 4,
        compiler_params=pltpu.CompilerParams(
            dimension_semantics=("arbitrary",), vmem_limit_bytes=VMEM_LIMIT_BYTES),
        name="l1_front",
    )(h, g, w_in, ln_g, ln_b, w_s, b_s)


LOG2E = math.log2(math.e)
WEIGHT_UNDERFLOW_LOG = -104.0


def _attn_kernel(q_ref, k_ref, v_ref, neg_upper_ref, o_ref, acc_ref, carry_ref):
    T, W = q_ref.shape
    n_pairs = W // V7X_LANES
    tq, tk = ATT_TQ, ATT_TK
    lane = lax.broadcasted_iota(jnp.int32, (tq, V7X_LANES), 1)
    first_head = lane < SB_DH
    row_q = lax.broadcasted_iota(jnp.int32, (2 * tq, 1), 0) & (tq - 1)
    col_k = lax.broadcasted_iota(jnp.int32, (1, tk), 1)

    causal = col_k < row_q

    def process(qs, chunks, fresh):
        pairs = range(n_pairs)
        cols = [slice(hp * V7X_LANES, (hp + 1) * V7X_LANES) for hp in pairs]
        chains = [(k0, mask, hp) for k0, mask in chunks for hp in pairs]
        z = [lax.dot_general(qs[hp], k_ref[pl.ds(k0, tk), cols[hp]], (((1,), (1,)), ((), ())),
                             preferred_element_type=F32) for k0, _, hp in chains]
        log_beta, keep_cost, row_sum = [], [], []
        for (_, mask, _), zc in zip(chains, z):
            softplus = jnp.maximum(zc, 0.0) + jnp.log(1.0 + jnp.exp2(jnp.abs(zc) * -LOG2E))
            log_beta.append(zc - softplus)
            if mask is not None:
                softplus = jnp.where(mask, softplus, 0.0)
            keep_cost.append(softplus.astype(BF16))
            row_sum.append(jnp.broadcast_to(jnp.sum(softplus, axis=1, keepdims=True),
                                            (2 * tq, V7X_LANES)))
        neg_upper = neg_upper_ref[...]
        suffix = [jnp.dot(kc, neg_upper, preferred_element_type=F32) for kc in keep_cost]
        carry = [None if fresh else carry_ref[hp] for hp in pairs]
        wgt = []
        for c, (_, mask, hp) in enumerate(chains):
            logw = log_beta[c] + suffix[c]
            if carry[hp] is not None:
                logw = logw + jnp.concatenate([carry[hp]] * (tk // V7X_LANES), axis=1)
            w = jnp.exp2(logw * LOG2E)
            if mask is not None:
                w = jnp.where(mask, w, 0.0)
            wgt.append(w.astype(BF16))
            carry[hp] = -row_sum[c] if carry[hp] is None else carry[hp] - row_sum[c]
        for hp in pairs:
            acc = None if fresh else acc_ref[hp]
            for c, (k0, _, chain_hp) in enumerate(chains):
                if chain_hp == hp:
                    pv = jnp.dot(wgt[c], v_ref[pl.ds(k0, tk), cols[hp]],
                                 preferred_element_type=F32)
                    acc = pv if acc is None else acc + pv
            acc_ref[hp] = acc
            carry_ref[hp] = carry[hp]

    def largest_carry():
        top = carry_ref[0]
        for hp in range(1, n_pairs):
            top = jnp.maximum(top, carry_ref[hp])
        return jnp.max(top)

    def q_block(i):
        is_first = isinstance(i, int)
        q0 = i * tq if is_first else pl.multiple_of(i * tq, tq)
        qs = []
        for hp in range(n_pairs):
            q2 = q_ref[pl.ds(q0, tq), hp * V7X_LANES:(hp + 1) * V7X_LANES]
            zero = jnp.zeros_like(q2)
            qs.append(jnp.concatenate(
                [jnp.where(first_head, q2, zero), jnp.where(first_head, zero, q2)], axis=0))
        if is_first:
            process(qs, [(q0, causal)], fresh=True)
        else:
            process(qs, [(q0, causal), (pl.multiple_of(q0 - tk, tk), None)], fresh=True)

            def more_to_do(state):
                t, top = state
                return jnp.logical_and(t < i - 1, top > WEIGHT_UNDERFLOW_LOG)

            def farther(state):
                t, _ = state
                process(qs, [(pl.multiple_of((i - 2 - t) * tk, tk), None)], fresh=False)
                return t + 1, largest_carry()

            lax.while_loop(more_to_do, farther, (jnp.int32(0), largest_carry()))
        for hp in range(n_pairs):
            acc = acc_ref[hp]
            o_ref[pl.ds(q0, tq), hp * V7X_LANES:(hp + 1) * V7X_LANES] = jnp.where(
                first_head, acc[:tq], acc[tq:]).astype(o_ref.dtype)

    assert tq == tk, "the diagonal mask and chunk walk assume square blocks"
    q_block(0)

    def later_block(i, _):
        q_block(i)
        return 0

    lax.fori_loop(1, T // tq, later_block, 0)


def _attention(q, k, v):
    B, T, W = q.shape
    neg_upper = -jnp.triu(jnp.ones((ATT_TK, ATT_TK), BF16), 1).T
    seq_spec = pl.BlockSpec((None, T, W), lambda b: (b, 0, 0))
    n_pairs = W // V7X_LANES
    return pl.pallas_call(
        _attn_kernel,
        out_shape=jax.ShapeDtypeStruct((B, T, W), BF16),
        grid=(B,),
        in_specs=[seq_spec, seq_spec, seq_spec, _resident(neg_upper.shape)],
        out_specs=seq_spec,
        scratch_shapes=[pltpu.VMEM((n_pairs, 2 * ATT_TQ, V7X_LANES), F32),
                        pltpu.VMEM((n_pairs, 2 * ATT_TQ, V7X_LANES), F32)],
        compiler_params=pltpu.CompilerParams(
            dimension_semantics=("arbitrary",), vmem_limit_bytes=VMEM_LIMIT_BYTES),
        name="stick_breaking_attention",
    )(q, k, v, neg_upper)


def kernel(x, mix_norm_g, mlp_norm_g, ab_w_in, pool_w, pool_scale, conv_w, conv_b, ab_w_out,
           cd_w_in, sgu_norm_g, sgu_norm_b, sgu_w, sgu_b, cd_w_out, mlp_w1, mlp_w2, final_norm_g):
    B, T, D = x.shape
    N = B * T
    row = lambda vec: vec.reshape(1, -1)

    h = _l0_mixer(x, row(mix_norm_g[0]), ab_w_in[0].astype(BF16), pool_w[0].astype(BF16),
                  pool_scale[0], conv_w[0], row(conv_b[0]), ab_w_out[0].astype(BF16))
    h = _mlp(h.reshape(N, D), row(mlp_norm_g[0]), mlp_w1[0].astype(BF16), mlp_w2[0].astype(BF16))

    b_s = jnp.broadcast_to(sgu_b[0][:, :, None], sgu_b[0].shape + (GROUP_WIDTH,))
    c, q, k, v = _l1_front(h, row(mix_norm_g[1]), cd_w_in[0].astype(BF16), row(sgu_norm_g[0]),
                           row(sgu_norm_b[0]), sgu_w[0], b_s)
    shape3 = (B, T, MIXER_WIDTH)
    d = _attention(q.reshape(shape3), k.reshape(shape3), v.reshape(shape3)).reshape(N, MIXER_WIDTH)
    out = _mix_mlp_final(h, c, d, cd_w_out[0].astype(BF16), row(mlp_norm_g[1]),
                         mlp_w1[1].astype(BF16), mlp_w2[1].astype(BF16), row(final_norm_g))
    return out.reshape(B, T, D)
```

```python
import math

import jax
import jax.numpy as jnp
from jax import lax
from jax.experimental import pallas as pl
from jax.experimental.pallas import tpu as pltpu

F32 = jnp.float32
BF16 = jnp.bfloat16

EPS = 1e-6
POOL_WINDOWS = (2, 4, 8, 16)
GROUP_WIDTH = 128
MIXER_WIDTH = 512
SGU_CHUNK = 128
SB_DH = 64

V7X_LANES = 128
V7X_SUBLANES = 8
V7X_VMEM_BYTES = 64 * 1024 * 1024
VMEM_LIMIT_BYTES = V7X_VMEM_BYTES - 8 * 1024 * 1024

ROW_TILE = 512
POOL_HALO = 16
CONV_HALO = V7X_SUBLANES
ATT_BLOCK = 256
MLP_FF_CHUNK = 1024

LOG2E = math.log2(math.e)
WEIGHT_UNDERFLOW_LOG = -104.0
CHUNK_OFF = -1e30


def _resident(shape):
    zeros = (0,) * len(shape)
    return pl.BlockSpec(shape, lambda *_: zeros, pipeline_mode=pl.Buffered(1))


def _rmsnorm(x, g):
    ms = jnp.mean(x * x, axis=-1, keepdims=True)
    return x * lax.rsqrt(ms + EPS) * g


def _trailing_window_sum(e, window):
    s = e
    span = 1
    while span < window:
        s = s + pltpu.roll(s, span, axis=0)
        span *= 2
    return s


def _l0_mixer_kernel(x_ref, g_ref, win_ref, poolw_ref, pscale_ref, convw_ref, convb_ref,
                     wout_ref, o_ref, a_carry, u_carry):
    j = pl.program_id(1)
    tm = x_ref.shape[0]

    @pl.when(j == 0)
    def _():
        a_carry[...] = jnp.zeros_like(a_carry)
        u_carry[...] = jnp.zeros_like(u_carry)

    x = x_ref[...]
    xn = _rmsnorm(x, g_ref[...]).astype(BF16)
    p = jnp.dot(xn, win_ref[...], preferred_element_type=F32)
    w = MIXER_WIDTH
    a, xb, gate_b, gate_c = p[:, :w], p[:, w:2 * w], p[:, 2 * w:3 * w], p[:, 3 * w:]

    a_ext = jnp.concatenate([a_carry[...], a], axis=0)
    a_carry[...] = a[tm - POOL_HALO:, :]
    pos = j * tm + lax.broadcasted_iota(jnp.int32, (tm, 1), 0)
    outs = []
    for grp, window in enumerate(POOL_WINDOWS):
        cols = slice(grp * GROUP_WIDTH, (grp + 1) * GROUP_WIDTH)
        wsum = _trailing_window_sum(a_ext[:, cols], window)[POOL_HALO:, :]
        count = jnp.minimum(pos + 1, window).astype(F32)
        pooled = wsum / count - a[:, cols]
        mixed = jnp.dot(pooled.astype(BF16), poolw_ref[grp], preferred_element_type=F32)
        outs.append(mixed * pscale_ref[grp:grp + 1, :])

    u = gate_c * xb
    u_ext = jnp.concatenate([u_carry[...], u], axis=0)
    u_carry[...] = u[tm - CONV_HALO:, :]
    u1 = pltpu.roll(u_ext, 1, axis=0)[CONV_HALO:, :]
    u2 = pltpu.roll(u_ext, 2, axis=0)[CONV_HALO:, :]
    y = u2 * convw_ref[0:1, :] + u1 * convw_ref[1:2, :] + u * convw_ref[2:3, :] + convb_ref[...]
    outs.append(gate_b * y)

    mix_in = jnp.concatenate(outs, axis=1).astype(BF16)
    o_ref[...] = x + jnp.dot(mix_in, wout_ref[...], preferred_element_type=F32)


def _l0_mixer(x, g, w_in, pool_w, pool_scale, conv_w, conv_b, w_out):
    B, T, D = x.shape
    tm = min(ROW_TILE, T)
    grid = (B, T // tm)
    row_spec = pl.BlockSpec((None, tm, D), lambda b, j: (b, j, 0))
    return pl.pallas_call(
        _l0_mixer_kernel,
        out_shape=jax.ShapeDtypeStruct((B, T, D), F32),
        grid=grid,
        in_specs=[row_spec, _resident(g.shape), _resident(w_in.shape), _resident(pool_w.shape),
                  _resident(pool_scale.shape), _resident(conv_w.shape), _resident(conv_b.shape),
                  _resident(w_out.shape)],
        out_specs=row_spec,
        scratch_shapes=[pltpu.VMEM((POOL_HALO, MIXER_WIDTH), F32),
                        pltpu.VMEM((CONV_HALO, MIXER_WIDTH), F32)],
        compiler_params=pltpu.CompilerParams(
            dimension_semantics=("arbitrary", "arbitrary"), vmem_limit_bytes=VMEM_LIMIT_BYTES),
        name="l0_mixer",
    )(x, g, w_in, pool_w, pool_scale, conv_w, conv_b, w_out)


def _mlp_kernel(h_ref, g_ref, w1_ref, w2_ref, o_ref):
    h = h_ref[...]
    d_ff = w1_ref.shape[1]
    chunk = min(MLP_FF_CHUNK, d_ff)
    hn = _rmsnorm(h, g_ref[...]).astype(BF16)
    acc = h
    for c in range(d_ff // chunk):
        cols = slice(c * chunk, (c + 1) * chunk)
        a = jnp.dot(hn, w1_ref[:, cols], preferred_element_type=F32)
        r = jnp.maximum(a, 0.0)
        acc = acc + jnp.dot((r * r).astype(BF16), w2_ref[cols, :], preferred_element_type=F32)
    o_ref[...] = acc


def _mlp(h, g, w1, w2):
    N, D = h.shape
    tm = min(ROW_TILE, N)
    row_spec = pl.BlockSpec((tm, D), lambda i: (i, 0))
    return pl.pallas_call(
        _mlp_kernel,
        out_shape=jax.ShapeDtypeStruct((N, D), F32),
        grid=(N // tm,),
        in_specs=[row_spec, _resident(g.shape), _resident(w1.shape), _resident(w2.shape)],
        out_specs=row_spec,
        compiler_params=pltpu.CompilerParams(
            dimension_semantics=("arbitrary",), vmem_limit_bytes=VMEM_LIMIT_BYTES),
        name="mlp0",
    )(h, g, w1, w2)


def _gelu(x):
    return 0.5 * x * (1.0 + lax.erf(x * (1.0 / math.sqrt(2.0))))


def _l1_front_kernel(h_ref, g_ref, win_ref, lng_ref, lnb_ref, ws_ref, bs_ref,
                     c_ref, q_ref, k_ref, v_ref):
    tm = h_ref.shape[0]
    w = MIXER_WIDTH
    xn = _rmsnorm(h_ref[...], g_ref[...]).astype(BF16)
    p = jnp.dot(xn, win_ref[...], preferred_element_type=F32)
    u = _gelu(p[:, :w])
    v = _gelu(p[:, w:2 * w])
    q_ref[...] = (p[:, 2 * w:3 * w] * (1.0 / math.sqrt(SB_DH))).astype(BF16)
    k_ref[...] = p[:, 3 * w:4 * w].astype(BF16)
    v_ref[...] = p[:, 4 * w:].astype(BF16)

    mu = jnp.mean(v, axis=-1, keepdims=True)
    vc = v - mu
    var = jnp.mean(vc * vc, axis=-1, keepdims=True)
    vn = (vc * lax.rsqrt(var + EPS) * lng_ref[...] + lnb_ref[...]).astype(BF16)

    L = SGU_CHUNK
    tpos = lax.broadcasted_iota(jnp.int32, (L, L), 0)
    spos = lax.broadcasted_iota(jnp.int32, (L, L), 1)
    causal = spos <= tpos
    for grp in range(w // GROUP_WIDTH):
        cols = slice(grp * GROUP_WIDTH, (grp + 1) * GROUP_WIDTH)
        w_s = jnp.where(causal, ws_ref[grp], 0.0).astype(BF16)
        bias = bs_ref[grp]
        for n in range(tm // L):
            rows = slice(n * L, (n + 1) * L)
            s = jnp.dot(w_s, vn[rows, cols], preferred_element_type=F32) + bias
            c_ref[rows, cols] = (u[rows, cols] * s).astype(BF16)


def _l1_front(h, g, w_in, ln_g, ln_b, w_s, b_s):
    N, D = h.shape
    tm = min(ROW_TILE, N)
    row_spec = pl.BlockSpec((tm, D), lambda i: (i, 0))
    half_spec = pl.BlockSpec((tm, MIXER_WIDTH), lambda i: (i, 0))
    half_shape = jax.ShapeDtypeStruct((N, MIXER_WIDTH), BF16)
    return pl.pallas_call(
        _l1_front_kernel,
        out_shape=(half_shape,) * 4,
        grid=(N // tm,),
        in_specs=[row_spec, _resident(g.shape), _resident(w_in.shape), _resident(ln_g.shape),
                  _resident(ln_b.shape), _resident(w_s.shape), _resident(b_s.shape)],
        out_specs=(half_spec,) * 4,
        compiler_params=pltpu.CompilerParams(
            dimension_semantics=("arbitrary",), vmem_limit_bytes=VMEM_LIMIT_BYTES),
        name="l1_front",
    )(h, g, w_in, ln_g, ln_b, w_s, b_s)


def _attn_mlp_kernel(q_ref, k_ref, v_ref, neg_upper_ref, h_ref, c_ref, wout_ref, g_ref, w1_ref,
                     w2_ref, fg_ref, o_ref, acc_ref, carry_ref, d_ref):
    s = pl.program_id(0)
    n_blocks = pl.num_programs(0) - 1
    blk = ATT_BLOCK
    T, W = k_ref.shape
    n_pairs = W // V7X_LANES
    pairs = range(n_pairs)
    cols = [slice(hp * V7X_LANES, (hp + 1) * V7X_LANES) for hp in pairs]
    lane = lax.broadcasted_iota(jnp.int32, (blk, V7X_LANES), 1)
    first_head = lane < SB_DH
    row_q = lax.broadcasted_iota(jnp.int32, (2 * blk, 1), 0) & (blk - 1)
    col_k = lax.broadcasted_iota(jnp.int32, (1, blk), 1)
    causal = col_k < row_q

    def attention_phases(qs, chunks, carry_in):
        chains = [(k0, mask, bias, hp) for k0, mask, bias in chunks for hp in pairs]
        st = {}

        def scores():
            z = [lax.dot_general(qs[hp], k_ref[pl.ds(k0, blk), cols[hp]],
                                 (((1,), (1,)), ((), ())), preferred_element_type=F32)
                 for k0, _, _, hp in chains]
            st["log_beta"], st["keep_cost"], st["row_sum"] = [], [], []
            for (_, mask, _, _), zc in zip(chains, z):
                softplus = jnp.maximum(zc, 0.0) + jnp.log(1.0 + jnp.exp2(jnp.abs(zc) * -LOG2E))
                st["log_beta"].append(zc - softplus)
                if mask is not None:
                    softplus = jnp.where(mask, softplus, 0.0)
                st["keep_cost"].append(softplus.astype(BF16))
                st["row_sum"].append(jnp.broadcast_to(
                    jnp.sum(softplus, axis=1, keepdims=True), (2 * blk, V7X_LANES)))

        def suffixes():
            neg_upper = neg_upper_ref[...]
            suffix = [jnp.dot(kc, neg_upper, preferred_element_type=F32)
                      for kc in st["keep_cost"]]
            carry = list(carry_in())
            st["wgt"] = []
            for c, (_, mask, bias, hp) in enumerate(chains):
                logw = st["log_beta"][c] + suffix[c]
                chunk_carry = carry[hp]
                if bias is not None:
                    chunk_carry = bias if chunk_carry is None else chunk_carry + bias
                if chunk_carry is not None:
                    logw = logw + jnp.concatenate([chunk_carry] * (blk // V7X_LANES), axis=1)
                w = jnp.exp2(logw * LOG2E)
                if mask is not None:
                    w = jnp.where(mask, w, 0.0)
                st["wgt"].append(w.astype(BF16))
                row_sum = st["row_sum"][c]
                carry[hp] = -row_sum if chunk_carry is None else chunk_carry - row_sum
            st["carry"] = carry

        def values():
            out = []
            for hp in pairs:
                acc = None
                for c, (k0, _, _, chain_hp) in enumerate(chains):
                    if chain_hp == hp:
                        pv = jnp.dot(st["wgt"][c], v_ref[pl.ds(k0, blk), cols[hp]],
                                     preferred_element_type=F32)
                        acc = pv if acc is None else acc + pv
                out.append((acc, st["carry"][hp]))
            return out

        return scores, suffixes, values, st

    def mlp_phases(slot):
        st = {}
        d_ff = w1_ref.shape[1]
        chunk = min(MLP_FF_CHUNK, d_ff)

        def out_proj():
            w = MIXER_WIDTH
            h = h_ref[...] + (jnp.dot(c_ref[...], wout_ref[:w, :], preferred_element_type=F32)
                              + jnp.dot(d_ref[slot], wout_ref[w:, :], preferred_element_type=F32))
            st["hn"] = _rmsnorm(h, g_ref[...]).astype(BF16)
            st["acc"] = h

        def up(c):
            def run():
                a = jnp.dot(st["hn"], w1_ref[:, c * chunk:(c + 1) * chunk],
                            preferred_element_type=F32)
                r = jnp.maximum(a, 0.0)
                st["r"] = (r * r).astype(BF16)
            return run

        def down(c):
            def run():
                st["acc"] = st["acc"] + jnp.dot(st["r"], w2_ref[c * chunk:(c + 1) * chunk, :],
                                                preferred_element_type=F32)
            return run

        def finish():
            o_ref[...] = _rmsnorm(st["acc"], fg_ref[...])

        steps = [out_proj]
        for c in range(d_ff // chunk):
            steps += [up(c), down(c)]
        return steps, finish

    def largest_carry():
        top = carry_ref[0]
        for hp in range(1, n_pairs):
            top = jnp.maximum(top, carry_ref[hp])
        return jnp.max(top)

    def run_step(with_attention, with_mlp):
        slot = lax.rem(s, 2)
        if with_mlp:
            mlp_steps, mlp_finish = mlp_phases(1 - slot)
        else:
            mlp_steps, mlp_finish = [], None
        if with_attention:
            i = lax.rem(s, T // blk)
            q0 = pl.multiple_of(i * blk, blk)
            qs = []
            for hp in pairs:
                q2 = q_ref[:, cols[hp]]
                zero = jnp.zeros_like(q2)
                qs.append(jnp.concatenate(
                    [jnp.where(first_head, q2, zero), jnp.where(first_head, zero, q2)], axis=0))
            prev0 = pl.multiple_of(jnp.maximum(q0 - blk, 0), blk)
            prev_bias = jnp.where(i == 0, CHUNK_OFF, 0.0).astype(F32)
            prev_bias = jnp.broadcast_to(prev_bias, (2 * blk, V7X_LANES))
            d_scores, d_suffixes, d_values, d_state = attention_phases(
                qs, [(q0, causal, None)], lambda: [None] * n_pairs)
            p_scores, p_suffixes, p_values, _ = attention_phases(
                qs, [(prev0, None, prev_bias)], lambda: [-rs for rs in d_state["row_sum"]])
            mlp_steps = list(mlp_steps)

            def mlp(count):
                for _ in range(min(count, len(mlp_steps))):
                    mlp_steps.pop(0)()

            d_scores()
            p_scores()
            mlp(3)
            d_suffixes()
            mlp(1)
            p_suffixes()
            mlp(2)
            d_out = d_values()
            mlp(1)
            for hp, ((d_acc, _), (p_acc, carry)) in enumerate(zip(d_out, p_values())):
                acc_ref[hp] = d_acc + p_acc
                carry_ref[hp] = carry
            mlp(len(mlp_steps))
        else:
            for step in mlp_steps:
                step()
        if with_mlp:
            mlp_finish()
        if with_attention:
            def more_to_do(state):
                t, top = state
                return jnp.logical_and(t < i - 1, top > WEIGHT_UNDERFLOW_LOG)

            def farther(state):
                t, _ = state
                k0 = pl.multiple_of((i - 2 - t) * blk, blk)
                sc, su, va, _ = attention_phases(qs, [(k0, None, None)],
                                                 lambda: [carry_ref[hp] for hp in pairs])
                sc()
                su()
                for hp, (pv, carry) in enumerate(va()):
                    acc_ref[hp] += pv
                    carry_ref[hp] = carry
                return t + 1, largest_carry()

            lax.while_loop(more_to_do, farther, (jnp.int32(0), largest_carry()))
            for hp in pairs:
                acc = acc_ref[hp]
                d_ref[slot, :, cols[hp]] = jnp.where(
                    first_head, acc[:blk], acc[blk:]).astype(d_ref.dtype)

    @pl.when(s == 0)
    def _():
        run_step(True, False)

    @pl.when(jnp.logical_and(s > 0, s < n_blocks))
    def _():
        run_step(True, True)

    @pl.when(s == n_blocks)
    def _():
        run_step(False, True)


def _attn_mlp_final(q, k, v, h, c, w_out, g, w1, w2, final_g):
    B, T, W = q.shape
    N, D = h.shape
    blk = ATT_BLOCK
    per_seq = T // blk
    n_blocks = N // blk
    neg_upper = -jnp.triu(jnp.ones((blk, blk), BF16), 1).T
    n_pairs = W // V7X_LANES

    def attn_block(s):
        return jnp.minimum(s, n_blocks - 1)

    def mlp_block(s):
        return jnp.maximum(s - 1, 0)

    q_spec = pl.BlockSpec((None, blk, W),
                          lambda s: (attn_block(s) // per_seq, attn_block(s) % per_seq, 0))
    kv_spec = pl.BlockSpec((None, T, W), lambda s: (attn_block(s) // per_seq, 0, 0))
    row_spec = pl.BlockSpec((blk, D), lambda s: (mlp_block(s), 0))
    half_spec = pl.BlockSpec((blk, W), lambda s: (mlp_block(s), 0))
    return pl.pallas_call(
        _attn_mlp_kernel,
        out_shape=jax.ShapeDtypeStruct((N, D), F32),
        grid=(n_blocks + 1,),
        in_specs=[q_spec, kv_spec, kv_spec, _resident(neg_upper.shape), row_spec, half_spec,
                  _resident(w_out.shape), _resident(g.shape), _resident(w1.shape),
                  _resident(w2.shape), _resident(final_g.shape)],
        out_specs=row_spec,
        scratch_shapes=[pltpu.VMEM((n_pairs, 2 * blk, V7X_LANES), F32),
                        pltpu.VMEM((n_pairs, 2 * blk, V7X_LANES), F32),
                        pltpu.VMEM((2, blk, W), BF16)],
        compiler_params=pltpu.CompilerParams(
            dimension_semantics=("arbitrary",), vmem_limit_bytes=VMEM_LIMIT_BYTES),
        name="attention_out_mlp1_final",
    )(q, k, v, neg_upper, h, c, w_out, g, w1, w2, final_g)


def kernel(x, mix_norm_g, mlp_norm_g, ab_w_in, pool_w, pool_scale, conv_w, conv_b, ab_w_out,
           cd_w_in, sgu_norm_g, sgu_norm_b, sgu_w, sgu_b, cd_w_out, mlp_w1, mlp_w2, final_norm_g):
    B, T, D = x.shape
    N = B * T
    row = lambda vec: vec.reshape(1, -1)

    h = _l0_mixer(x, row(mix_norm_g[0]), ab_w_in[0].astype(BF16), pool_w[0].astype(BF16),
                  pool_scale[0], conv_w[0], row(conv_b[0]), ab_w_out[0].astype(BF16))
    h = _mlp(h.reshape(N, D), row(mlp_norm_g[0]), mlp_w1[0].astype(BF16), mlp_w2[0].astype(BF16))

    b_s = jnp.broadcast_to(sgu_b[0][:, :, None], sgu_b[0].shape + (GROUP_WIDTH,))
    c, q, k, v = _l1_front(h, row(mix_norm_g[1]), cd_w_in[0].astype(BF16), row(sgu_norm_g[0]),
                           row(sgu_norm_b[0]), sgu_w[0], b_s)
    shape3 = (B, T, MIXER_WIDTH)
    out = _attn_mlp_final(q.reshape(shape3), k.reshape(shape3), v.reshape(shape3), h, c,
                          cd_w_out[0].astype(BF16), row(mlp_norm_g[1]), mlp_w1[1].astype(BF16),
                          mlp_w2[1].astype(BF16), row(final_norm_g))
    return out.reshape(B, T, D)
```

```python
import math

import jax
import jax.numpy as jnp
from jax import lax
from jax.experimental import pallas as pl
from jax.experimental.pallas import tpu as pltpu

F32 = jnp.float32
BF16 = jnp.bfloat16

EPS = 1e-6
POOL_WINDOWS = (2, 4, 8, 16)
GROUP_WIDTH = 128
MIXER_WIDTH = 512
SGU_CHUNK = 128
SB_DH = 64

V7X_LANES = 128
V7X_SUBLANES = 8
V7X_VMEM_BYTES = 64 * 1024 * 1024
VMEM_LIMIT_BYTES = V7X_VMEM_BYTES - 8 * 1024 * 1024

ROW_TILE = 1024
POOL_HALO = 16
CONV_HALO = V7X_SUBLANES
ATT_BLOCK = 256
MLP_FF_CHUNK = 1024

LOG2E = math.log2(math.e)
WEIGHT_UNDERFLOW_LOG = -104.0
CHUNK_OFF = -1e30


def _resident(shape):
    zeros = (0,) * len(shape)
    return pl.BlockSpec(shape, lambda *_: zeros, pipeline_mode=pl.Buffered(1))


def _rmsnorm(x, g):
    ms = jnp.mean(x * x, axis=-1, keepdims=True)
    return x * lax.rsqrt(ms + EPS) * g


def _trailing_window_sum(e, window):
    s = e
    span = 1
    while span < window:
        s = s + pltpu.roll(s, span, axis=0)
        span *= 2
    return s


def _l0_mixer_kernel(x_ref, g_ref, win_ref, poolw_ref, pscale_ref, convw_ref, convb_ref,
                     wout_ref, o_ref, a_carry, u_carry):
    j = pl.program_id(1)
    tm = x_ref.shape[0]

    @pl.when(j == 0)
    def _():
        a_carry[...] = jnp.zeros_like(a_carry)
        u_carry[...] = jnp.zeros_like(u_carry)

    x = x_ref[...]
    xn = _rmsnorm(x, g_ref[...]).astype(BF16)
    p = jnp.dot(xn, win_ref[...], preferred_element_type=F32)
    w = MIXER_WIDTH
    a, xb, gate_b, gate_c = p[:, :w], p[:, w:2 * w], p[:, 2 * w:3 * w], p[:, 3 * w:]

    a_ext = jnp.concatenate([a_carry[...], a], axis=0)
    a_carry[...] = a[tm - POOL_HALO:, :]
    pos = j * tm + lax.broadcasted_iota(jnp.int32, (tm, 1), 0)
    outs = []
    for grp, window in enumerate(POOL_WINDOWS):
        cols = slice(grp * GROUP_WIDTH, (grp + 1) * GROUP_WIDTH)
        wsum = _trailing_window_sum(a_ext[:, cols], window)[POOL_HALO:, :]
        count = jnp.minimum(pos + 1, window).astype(F32)
        pooled = wsum / count - a[:, cols]
        mixed = jnp.dot(pooled.astype(BF16), poolw_ref[grp], preferred_element_type=F32)
        outs.append(mixed * pscale_ref[grp:grp + 1, :])

    u = gate_c * xb
    u_ext = jnp.concatenate([u_carry[...], u], axis=0)
    u_carry[...] = u[tm - CONV_HALO:, :]
    u1 = pltpu.roll(u_ext, 1, axis=0)[CONV_HALO:, :]
    u2 = pltpu.roll(u_ext, 2, axis=0)[CONV_HALO:, :]
    y = u2 * convw_ref[0:1, :] + u1 * convw_ref[1:2, :] + u * convw_ref[2:3, :] + convb_ref[...]
    outs.append(gate_b * y)

    mix_in = jnp.concatenate(outs, axis=1).astype(BF16)
    o_ref[...] = x + jnp.dot(mix_in, wout_ref[...], preferred_element_type=F32)


def _l0_mixer(x, g, w_in, pool_w, pool_scale, conv_w, conv_b, w_out):
    B, T, D = x.shape
    tm = min(ROW_TILE, T)
    grid = (B, T // tm)
    row_spec = pl.BlockSpec((None, tm, D), lambda b, j: (b, j, 0))
    return pl.pallas_call(
        _l0_mixer_kernel,
        out_shape=jax.ShapeDtypeStruct((B, T, D), F32),
        grid=grid,
        in_specs=[row_spec, _resident(g.shape), _resident(w_in.shape), _resident(pool_w.shape),
                  _resident(pool_scale.shape), _resident(conv_w.shape), _resident(conv_b.shape),
                  _resident(w_out.shape)],
        out_specs=row_spec,
        scratch_shapes=[pltpu.VMEM((POOL_HALO, MIXER_WIDTH), F32),
                        pltpu.VMEM((CONV_HALO, MIXER_WIDTH), F32)],
        compiler_params=pltpu.CompilerParams(
            dimension_semantics=("arbitrary", "arbitrary"), vmem_limit_bytes=VMEM_LIMIT_BYTES),
        name="l0_mixer",
    )(x, g, w_in, pool_w, pool_scale, conv_w, conv_b, w_out)


def _mlp_kernel(h_ref, g_ref, w1_ref, w2_ref, o_ref):
    h = h_ref[...]
    d_ff = w1_ref.shape[1]
    chunk = min(MLP_FF_CHUNK, d_ff)
    hn = _rmsnorm(h, g_ref[...]).astype(BF16)
    acc = h
    for c in range(d_ff // chunk):
        cols = slice(c * chunk, (c + 1) * chunk)
        a = jnp.dot(hn, w1_ref[:, cols], preferred_element_type=F32)
        r = jnp.maximum(a, 0.0)
        acc = acc + jnp.dot((r * r).astype(BF16), w2_ref[cols, :], preferred_element_type=F32)
    o_ref[...] = acc


def _mlp(h, g, w1, w2):
    N, D = h.shape
    tm = min(ROW_TILE, N)
    row_spec = pl.BlockSpec((tm, D), lambda i: (i, 0))
    return pl.pallas_call(
        _mlp_kernel,
        out_shape=jax.ShapeDtypeStruct((N, D), F32),
        grid=(N // tm,),
        in_specs=[row_spec, _resident(g.shape), _resident(w1.shape), _resident(w2.shape)],
        out_specs=row_spec,
        compiler_params=pltpu.CompilerParams(
            dimension_semantics=("arbitrary",), vmem_limit_bytes=VMEM_LIMIT_BYTES),
        name="mlp0",
    )(h, g, w1, w2)


def _gelu(x):
    return 0.5 * x * (1.0 + lax.erf(x * (1.0 / math.sqrt(2.0))))


def _l1_front_kernel(h_ref, g_ref, win_ref, lng_ref, lnb_ref, ws_ref, bs_ref,
                     c_ref, q_ref, k_ref, v_ref):
    tm = h_ref.shape[0]
    w = MIXER_WIDTH
    xn = _rmsnorm(h_ref[...], g_ref[...]).astype(BF16)
    p = jnp.dot(xn, win_ref[...], preferred_element_type=F32)
    u = _gelu(p[:, :w])
    v = _gelu(p[:, w:2 * w])
    q_ref[...] = (p[:, 2 * w:3 * w] * (1.0 / math.sqrt(SB_DH))).astype(BF16)
    k_ref[...] = p[:, 3 * w:4 * w].astype(BF16)
    v_ref[...] = p[:, 4 * w:].astype(BF16)

    mu = jnp.mean(v, axis=-1, keepdims=True)
    vc = v - mu
    var = jnp.mean(vc * vc, axis=-1, keepdims=True)
    vn = (vc * lax.rsqrt(var + EPS) * lng_ref[...] + lnb_ref[...]).astype(BF16)

    L = SGU_CHUNK
    tpos = lax.broadcasted_iota(jnp.int32, (L, L), 0)
    spos = lax.broadcasted_iota(jnp.int32, (L, L), 1)
    causal = spos <= tpos
    for grp in range(w // GROUP_WIDTH):
        cols = slice(grp * GROUP_WIDTH, (grp + 1) * GROUP_WIDTH)
        w_s = jnp.where(causal, ws_ref[grp], 0.0).astype(BF16)
        bias = bs_ref[grp]
        for n in range(tm // L):
            rows = slice(n * L, (n + 1) * L)
            s = jnp.dot(w_s, vn[rows, cols], preferred_element_type=F32) + bias
            c_ref[rows, cols] = (u[rows, cols] * s).astype(BF16)


def _l1_front(h, g, w_in, ln_g, ln_b, w_s, b_s):
    N, D = h.shape
    tm = min(ROW_TILE, N)
    row_spec = pl.BlockSpec((tm, D), lambda i: (i, 0))
    half_spec = pl.BlockSpec((tm, MIXER_WIDTH), lambda i: (i, 0))
    half_shape = jax.ShapeDtypeStruct((N, MIXER_WIDTH), BF16)
    return pl.pallas_call(
        _l1_front_kernel,
        out_shape=(half_shape,) * 4,
        grid=(N // tm,),
        in_specs=[row_spec, _resident(g.shape), _resident(w_in.shape), _resident(ln_g.shape),
                  _resident(ln_b.shape), _resident(w_s.shape), _resident(b_s.shape)],
        out_specs=(half_spec,) * 4,
        compiler_params=pltpu.CompilerParams(
            dimension_semantics=("arbitrary",), vmem_limit_bytes=VMEM_LIMIT_BYTES),
        name="l1_front",
    )(h, g, w_in, ln_g, ln_b, w_s, b_s)


def _attn_mlp_kernel(q_ref, k_ref, v_ref, neg_upper_ref, h_ref, c_ref, wout_ref, g_ref, w1_ref,
                     w2_ref, fg_ref, o_ref, acc_ref, carry_ref, d_ref):
    s = pl.program_id(0)
    n_blocks = pl.num_programs(0) - 1
    blk = ATT_BLOCK
    T, W = k_ref.shape
    n_pairs = W // V7X_LANES
    pairs = range(n_pairs)
    cols = [slice(hp * V7X_LANES, (hp + 1) * V7X_LANES) for hp in pairs]
    lane = lax.broadcasted_iota(jnp.int32, (blk, V7X_LANES), 1)
    first_head = lane < SB_DH
    row_q = lax.broadcasted_iota(jnp.int32, (2 * blk, 1), 0) & (blk - 1)
    col_k = lax.broadcasted_iota(jnp.int32, (1, blk), 1)
    causal = col_k < row_q

    def attention_phases(qs, chunks, carry_in):
        chains = [(k0, mask, bias, hp) for k0, mask, bias in chunks for hp in pairs]
        st = {}

        def scores():
            z = [lax.dot_general(qs[hp], k_ref[pl.ds(k0, blk), cols[hp]],
                                 (((1,), (1,)), ((), ())), preferred_element_type=F32)
                 for k0, _, _, hp in chains]
            st["log_beta"], st["keep_cost"], st["row_sum"] = [], [], []
            for (_, mask, _, _), zc in zip(chains, z):
                softplus = jnp.maximum(zc, 0.0) + jnp.log(1.0 + jnp.exp2(jnp.abs(zc) * -LOG2E))
                st["log_beta"].append(zc - softplus)
                if mask is not None:
                    softplus = jnp.where(mask, softplus, 0.0)
                st["keep_cost"].append(softplus.astype(BF16))
                st["row_sum"].append(jnp.broadcast_to(
                    jnp.sum(softplus, axis=1, keepdims=True), (2 * blk, V7X_LANES)))

        def suffixes():
            neg_upper = neg_upper_ref[...]
            suffix = [jnp.dot(kc, neg_upper, preferred_element_type=F32)
                      for kc in st["keep_cost"]]
            carry = list(carry_in())
            st["wgt"] = []
            for c, (_, mask, bias, hp) in enumerate(chains):
                logw = st["log_beta"][c] + suffix[c]
                chunk_carry = carry[hp]
                if bias is not None:
                    chunk_carry = bias if chunk_carry is None else chunk_carry + bias
                if chunk_carry is not None:
                    logw = logw + jnp.concatenate([chunk_carry] * (blk // V7X_LANES), axis=1)
                w = jnp.exp2(logw * LOG2E)
                if mask is not None:
                    w = jnp.where(mask, w, 0.0)
                st["wgt"].append(w.astype(BF16))
                row_sum = st["row_sum"][c]
                carry[hp] = -row_sum if chunk_carry is None else chunk_carry - row_sum
            st["carry"] = carry

        def values():
            out = []
            for hp in pairs:
                acc = None
                for c, (k0, _, _, chain_hp) in enumerate(chains):
                    if chain_hp == hp:
                        pv = jnp.dot(st["wgt"][c], v_ref[pl.ds(k0, blk), cols[hp]],
                                     preferred_element_type=F32)
                        acc = pv if acc is None else acc + pv
                out.append((acc, st["carry"][hp]))
            return out

        return scores, suffixes, values, st

    def mlp_phases(slot):
        st = {}
        d_ff = w1_ref.shape[1]
        chunk = min(MLP_FF_CHUNK, d_ff)

        def out_proj():
            w = MIXER_WIDTH
            h = h_ref[...] + (jnp.dot(c_ref[...], wout_ref[:w, :], preferred_element_type=F32)
                              + jnp.dot(d_ref[slot], wout_ref[w:, :], preferred_element_type=F32))
            st["hn"] = _rmsnorm(h, g_ref[...]).astype(BF16)
            st["acc"] = h

        def up(c):
            def run():
                a = jnp.dot(st["hn"], w1_ref[:, c * chunk:(c + 1) * chunk],
                            preferred_element_type=F32)
                r = jnp.maximum(a, 0.0)
                st["r"] = (r * r).astype(BF16)
            return run

        def down(c):
            def run():
                st["acc"] = st["acc"] + jnp.dot(st["r"], w2_ref[c * chunk:(c + 1) * chunk, :],
                                                preferred_element_type=F32)
            return run

        def finish():
            o_ref[...] = _rmsnorm(st["acc"], fg_ref[...])

        steps = [out_proj]
        for c in range(d_ff // chunk):
            steps += [up(c), down(c)]
        return steps, finish

    def largest_carry():
        top = carry_ref[0]
        for hp in range(1, n_pairs):
            top = jnp.maximum(top, carry_ref[hp])
        return jnp.max(top)

    def run_step(with_attention, with_mlp):
        slot = lax.rem(s, 2)
        if with_mlp:
            mlp_steps, mlp_finish = mlp_phases(1 - slot)
        else:
            mlp_steps, mlp_finish = [], None
        if with_attention:
            i = lax.rem(s, T // blk)
            q0 = pl.multiple_of(i * blk, blk)
            qs = []
            for hp in pairs:
                q2 = q_ref[:, cols[hp]]
                zero = jnp.zeros_like(q2)
                qs.append(jnp.concatenate(
                    [jnp.where(first_head, q2, zero), jnp.where(first_head, zero, q2)], axis=0))
            prev0 = pl.multiple_of(jnp.maximum(q0 - blk, 0), blk)
            prev_bias = jnp.where(i == 0, CHUNK_OFF, 0.0).astype(F32)
            prev_bias = jnp.broadcast_to(prev_bias, (2 * blk, V7X_LANES))
            d_scores, d_suffixes, d_values, d_state = attention_phases(
                qs, [(q0, causal, None)], lambda: [None] * n_pairs)
            p_scores, p_suffixes, p_values, _ = attention_phases(
                qs, [(prev0, None, prev_bias)], lambda: [-rs for rs in d_state["row_sum"]])
            mlp_steps = list(mlp_steps)

            def mlp(count):
                for _ in range(min(count, len(mlp_steps))):
                    mlp_steps.pop(0)()

            d_scores()
            p_scores()
            mlp(3)
            d_suffixes()
            mlp(1)
            p_suffixes()
            mlp(2)
            d_out = d_values()
            mlp(1)
            for hp, ((d_acc, _), (p_acc, carry)) in enumerate(zip(d_out, p_values())):
                acc_ref[hp] = d_acc + p_acc
                carry_ref[hp] = carry
            mlp(len(mlp_steps))
        else:
            for step in mlp_steps:
                step()
        if with_mlp:
            mlp_finish()
        if with_attention:
            def more_to_do(state):
                t, top = state
                return jnp.logical_and(t < i - 1, top > WEIGHT_UNDERFLOW_LOG)

            def farther(state):
                t, _ = state
                k0 = pl.multiple_of((i - 2 - t) * blk, blk)
                sc, su, va, _ = attention_phases(qs, [(k0, None, None)],
                                                 lambda: [carry_ref[hp] for hp in pairs])
                sc()
                su()
                for hp, (pv, carry) in enumerate(va()):
                    acc_ref[hp] += pv
                    carry_ref[hp] = carry
                return t + 1, largest_carry()

            lax.while_loop(more_to_do, farther, (jnp.int32(0), largest_carry()))
            for hp in pairs:
                acc = acc_ref[hp]
                d_ref[slot, :, cols[hp]] = jnp.where(
                    first_head, acc[:blk], acc[blk:]).astype(d_ref.dtype)

    @pl.when(s == 0)
    def _():
        run_step(True, False)

    @pl.when(jnp.logical_and(s > 0, s < n_blocks))
    def _():
        run_step(True, True)

    @pl.when(s == n_blocks)
    def _():
        run_step(False, True)


def _attn_mlp_final(q, k, v, h, c, w_out, g, w1, w2, final_g):
    B, T, W = q.shape
    N, D = h.shape
    blk = ATT_BLOCK
    per_seq = T // blk
    n_blocks = N // blk
    neg_upper = -jnp.triu(jnp.ones((blk, blk), BF16), 1).T
    n_pairs = W // V7X_LANES

    def attn_block(s):
        return jnp.minimum(s, n_blocks - 1)

    def mlp_block(s):
        return jnp.maximum(s - 1, 0)

    q_spec = pl.BlockSpec((None, blk, W),
                          lambda s: (attn_block(s) // per_seq, attn_block(s) % per_seq, 0))
    kv_spec = pl.BlockSpec((None, T, W), lambda s: (attn_block(s) // per_seq, 0, 0))
    row_spec = pl.BlockSpec((blk, D), lambda s: (mlp_block(s), 0))
    half_spec = pl.BlockSpec((blk, W), lambda s: (mlp_block(s), 0))
    return pl.pallas_call(
        _attn_mlp_kernel,
        out_shape=jax.ShapeDtypeStruct((N, D), F32),
        grid=(n_blocks + 1,),
        in_specs=[q_spec, kv_spec, kv_spec, _resident(neg_upper.shape), row_spec, half_spec,
                  _resident(w_out.shape), _resident(g.shape), _resident(w1.shape),
                  _resident(w2.shape), _resident(final_g.shape)],
        out_specs=row_spec,
        scratch_shapes=[pltpu.VMEM((n_pairs, 2 * blk, V7X_LANES), F32),
                        pltpu.VMEM((n_pairs, 2 * blk, V7X_LANES), F32),
                        pltpu.VMEM((2, blk, W), BF16)],
        compiler_params=pltpu.CompilerParams(
            dimension_semantics=("arbitrary",), vmem_limit_bytes=VMEM_LIMIT_BYTES),
        name="attention_out_mlp1_final",
    )(q, k, v, neg_upper, h, c, w_out, g, w1, w2, final_g)


def kernel(x, mix_norm_g, mlp_norm_g, ab_w_in, pool_w, pool_scale, conv_w, conv_b, ab_w_out,
           cd_w_in, sgu_norm_g, sgu_norm_b, sgu_w, sgu_b, cd_w_out, mlp_w1, mlp_w2, final_norm_g):
    B, T, D = x.shape
    N = B * T
    row = lambda vec: vec.reshape(1, -1)

    h = _l0_mixer(x, row(mix_norm_g[0]), ab_w_in[0].astype(BF16), pool_w[0].astype(BF16),
                  pool_scale[0], conv_w[0], row(conv_b[0]), ab_w_out[0].astype(BF16))
    h = _mlp(h.reshape(N, D), row(mlp_norm_g[0]), mlp_w1[0].astype(BF16), mlp_w2[0].astype(BF16))

    b_s = jnp.broadcast_to(sgu_b[0][:, :, None], sgu_b[0].shape + (GROUP_WIDTH,))
    c, q, k, v = _l1_front(h, row(mix_norm_g[1]), cd_w_in[0].astype(BF16), row(sgu_norm_g[0]),
                           row(sgu_norm_b[0]), sgu_w[0], b_s)
    shape3 = (B, T, MIXER_WIDTH)
    out = _attn_mlp_final(q.reshape(shape3), k.reshape(shape3), v.reshape(shape3), h, c,
                          cd_w_out[0].astype(BF16), row(mlp_norm_g[1]), mlp_w1[1].astype(BF16),
                          mlp_w2[1].astype(BF16), row(final_norm_g))
    return out.reshape(B, T, D)
```

```python
import math

import jax
import jax.numpy as jnp
from jax import lax
from jax.experimental import pallas as pl
from jax.experimental.pallas import tpu as pltpu

F32 = jnp.float32
BF16 = jnp.bfloat16

EPS = 1e-6
POOL_WINDOWS = (2, 4, 8, 16)
GROUP_WIDTH = 128
MIXER_WIDTH = 512
SGU_CHUNK = 128
SB_DH = 64

V7X_LANES = 128
V7X_SUBLANES = 8
V7X_VMEM_BYTES = 64 * 1024 * 1024
VMEM_LIMIT_BYTES = V7X_VMEM_BYTES - 8 * 1024 * 1024

ROW_TILE = 1024
POOL_HALO = 16
CONV_HALO = V7X_SUBLANES
ATT_BLOCK = 256
MLP_FF_CHUNK = 1024

LOG2E = math.log2(math.e)
WEIGHT_UNDERFLOW_LOG = -104.0
CHUNK_OFF = -1e30


def _resident(shape):
    zeros = (0,) * len(shape)
    return pl.BlockSpec(shape, lambda *_: zeros, pipeline_mode=pl.Buffered(1))


def _resident_layer(stacked_shape, layer):
    zeros = (0,) * (len(stacked_shape) - 1)
    return pl.BlockSpec((None,) + tuple(stacked_shape[1:]), lambda *_: (layer,) + zeros,
                        pipeline_mode=pl.Buffered(1))


def _rmsnorm(x, g):
    ms = jnp.mean(x * x, axis=-1, keepdims=True)
    return x * lax.rsqrt(ms + EPS) * g


def _trailing_window_sum(e, window):
    s = e
    span = 1
    while span < window:
        s = s + pltpu.roll(s, span, axis=0)
        span *= 2
    return s


def _l0_mixer_kernel(x_ref, g_ref, win_ref, poolw_ref, pscale_ref, convw_ref, convb_ref,
                     wout_ref, o_ref, a_carry, u_carry, wmix_ref):
    j = pl.program_id(1)
    tm = x_ref.shape[0]
    w = MIXER_WIDTH

    @pl.when(jnp.logical_and(pl.program_id(0) == 0, j == 0))
    def _():
        for grp in range(len(POOL_WINDOWS)):
            rows = slice(grp * GROUP_WIDTH, (grp + 1) * GROUP_WIDTH)
            scaled = poolw_ref[grp] * pscale_ref[grp:grp + 1, :]
            wmix_ref[rows, :] = jnp.dot(scaled, wout_ref[rows, :], preferred_element_type=F32,
                                        precision=lax.Precision.HIGHEST).astype(BF16)
        wmix_ref[w:, :] = wout_ref[w:, :].astype(BF16)

    @pl.when(j == 0)
    def _():
        a_carry[...] = jnp.zeros_like(a_carry)
        u_carry[...] = jnp.zeros_like(u_carry)

    x = x_ref[...]
    xn = _rmsnorm(x, g_ref[...]).astype(BF16)
    p = jnp.dot(xn, win_ref[...], preferred_element_type=F32)
    a, xb, gate_b, gate_c = p[:, :w], p[:, w:2 * w], p[:, 2 * w:3 * w], p[:, 3 * w:]

    a_ext = jnp.concatenate([a_carry[...], a], axis=0)
    a_carry[...] = a[tm - POOL_HALO:, :]
    pos = j * tm + lax.broadcasted_iota(jnp.int32, (tm, 1), 0)
    outs = []
    for grp, window in enumerate(POOL_WINDOWS):
        cols = slice(grp * GROUP_WIDTH, (grp + 1) * GROUP_WIDTH)
        wsum = _trailing_window_sum(a_ext[:, cols], window)[POOL_HALO:, :]
        count = jnp.minimum(pos + 1, window).astype(F32)
        outs.append(wsum / count - a[:, cols])

    u = gate_c * xb
    u_ext = jnp.concatenate([u_carry[...], u], axis=0)
    u_carry[...] = u[tm - CONV_HALO:, :]
    u1 = pltpu.roll(u_ext, 1, axis=0)[CONV_HALO:, :]
    u2 = pltpu.roll(u_ext, 2, axis=0)[CONV_HALO:, :]
    y = u2 * convw_ref[0:1, :] + u1 * convw_ref[1:2, :] + u * convw_ref[2:3, :] + convb_ref[...]
    outs.append(gate_b * y)

    mix_in = jnp.concatenate(outs, axis=1).astype(BF16)
    o_ref[...] = x + jnp.dot(mix_in, wmix_ref[...], preferred_element_type=F32)


def _l0_mixer(x, g, w_in, pool_w, pool_scale, conv_w, conv_b, w_out):
    B, T, D = x.shape
    tm = min(ROW_TILE, T)
    grid = (B, T // tm)
    row_spec = pl.BlockSpec((None, tm, D), lambda b, j: (b, j, 0))
    return pl.pallas_call(
        _l0_mixer_kernel,
        out_shape=jax.ShapeDtypeStruct((B, T, D), F32),
        grid=grid,
        in_specs=[row_spec, _resident(g.shape), _resident(w_in.shape), _resident(pool_w.shape),
                  _resident(pool_scale.shape), _resident(conv_w.shape), _resident(conv_b.shape),
                  _resident(w_out.shape)],
        out_specs=row_spec,
        scratch_shapes=[pltpu.VMEM((POOL_HALO, MIXER_WIDTH), F32),
                        pltpu.VMEM((CONV_HALO, MIXER_WIDTH), F32),
                        pltpu.VMEM(w_out.shape, BF16)],
        compiler_params=pltpu.CompilerParams(
            dimension_semantics=("arbitrary", "arbitrary"), vmem_limit_bytes=VMEM_LIMIT_BYTES),
        name="l0_mixer",
    )(x, g, w_in, pool_w, pool_scale, conv_w, conv_b, w_out)


def _mlp_kernel(h_ref, g_ref, w1_ref, w2_ref, o_ref):
    h = h_ref[...]
    d_ff = w1_ref.shape[1]
    chunk = min(MLP_FF_CHUNK, d_ff)
    hn = _rmsnorm(h, g_ref[...]).astype(BF16)
    acc = h
    for c in range(d_ff // chunk):
        cols = slice(c * chunk, (c + 1) * chunk)
        a = jnp.dot(hn, w1_ref[:, cols], preferred_element_type=F32)
        r = jnp.maximum(a, 0.0)
        acc = acc + jnp.dot((r * r).astype(BF16), w2_ref[cols, :], preferred_element_type=F32)
    o_ref[...] = acc


def _mlp(h, g, w1_layers, w2_layers, layer):
    N, D = h.shape
    tm = min(ROW_TILE, N)
    row_spec = pl.BlockSpec((tm, D), lambda i: (i, 0))
    return pl.pallas_call(
        _mlp_kernel,
        out_shape=jax.ShapeDtypeStruct((N, D), F32),
        grid=(N // tm,),
        in_specs=[row_spec, _resident(g.shape), _resident_layer(w1_layers.shape, layer),
                  _resident_layer(w2_layers.shape, layer)],
        out_specs=row_spec,
        compiler_params=pltpu.CompilerParams(
            dimension_semantics=("arbitrary",), vmem_limit_bytes=VMEM_LIMIT_BYTES),
        name="mlp0",
    )(h, g, w1_layers, w2_layers)


def _gelu(x):
    return 0.5 * x * (1.0 + lax.erf(x * (1.0 / math.sqrt(2.0))))


def _l1_front_kernel(h_ref, g_ref, win_ref, lng_ref, lnb_ref, ws_ref, bs_ref,
                     c_ref, q_ref, k_ref, v_ref):
    tm = h_ref.shape[0]
    w = MIXER_WIDTH
    xn = _rmsnorm(h_ref[...], g_ref[...]).astype(BF16)
    p = jnp.dot(xn, win_ref[...], preferred_element_type=F32)
    u = _gelu(p[:, :w])
    v = _gelu(p[:, w:2 * w])
    q_ref[...] = (p[:, 2 * w:3 * w] * (1.0 / math.sqrt(SB_DH))).astype(BF16)
    k_ref[...] = p[:, 3 * w:4 * w].astype(BF16)
    v_ref[...] = p[:, 4 * w:].astype(BF16)

    mu = jnp.mean(v, axis=-1, keepdims=True)
    vc = v - mu
    var = jnp.mean(vc * vc, axis=-1, keepdims=True)
    vn = (vc * lax.rsqrt(var + EPS) * lng_ref[...] + lnb_ref[...]).astype(BF16)

    L = SGU_CHUNK
    tpos = lax.broadcasted_iota(jnp.int32, (L, L), 0)
    spos = lax.broadcasted_iota(jnp.int32, (L, L), 1)
    causal = spos <= tpos
    for grp in range(w // GROUP_WIDTH):
        cols = slice(grp * GROUP_WIDTH, (grp + 1) * GROUP_WIDTH)
        w_s = jnp.where(causal, ws_ref[grp], 0.0).astype(BF16)
        bias = bs_ref[grp]
        for n in range(tm // L):
            rows = slice(n * L, (n + 1) * L)
            s = jnp.dot(w_s, vn[rows, cols], preferred_element_type=F32) + bias
            c_ref[rows, cols] = (u[rows, cols] * s).astype(BF16)


def _l1_front(h, g, w_in, ln_g, ln_b, w_s, b_s):
    N, D = h.shape
    tm = min(ROW_TILE, N)
    row_spec = pl.BlockSpec((tm, D), lambda i: (i, 0))
    half_spec = pl.BlockSpec((tm, MIXER_WIDTH), lambda i: (i, 0))
    half_shape = jax.ShapeDtypeStruct((N, MIXER_WIDTH), BF16)
    return pl.pallas_call(
        _l1_front_kernel,
        out_shape=(half_shape,) * 4,
        grid=(N // tm,),
        in_specs=[row_spec, _resident(g.shape), _resident(w_in.shape), _resident(ln_g.shape),
                  _resident(ln_b.shape), _resident(w_s.shape), _resident(b_s.shape)],
        out_specs=(half_spec,) * 4,
        compiler_params=pltpu.CompilerParams(
            dimension_semantics=("arbitrary",), vmem_limit_bytes=VMEM_LIMIT_BYTES),
        name="l1_front",
    )(h, g, w_in, ln_g, ln_b, w_s, b_s)


def _attn_mlp_kernel(q_ref, k_ref, v_ref, neg_upper_ref, h_ref, c_ref, wout_ref, g_ref, w1_ref,
                     w2_ref, fg_ref, o_ref, acc_ref, carry_ref, d_ref):
    s = pl.program_id(0)
    n_blocks = pl.num_programs(0) - 1
    blk = ATT_BLOCK
    T, W = k_ref.shape
    n_pairs = W // V7X_LANES
    pairs = range(n_pairs)
    cols = [slice(hp * V7X_LANES, (hp + 1) * V7X_LANES) for hp in pairs]
    lane = lax.broadcasted_iota(jnp.int32, (blk, V7X_LANES), 1)
    first_head = lane < SB_DH
    row_q = lax.broadcasted_iota(jnp.int32, (2 * blk, 1), 0) & (blk - 1)
    col_k = lax.broadcasted_iota(jnp.int32, (1, blk), 1)
    causal = col_k < row_q

    def attention_phases(qs, chunks, carry_in):
        chains = [(k0, mask, bias, hp) for k0, mask, bias in chunks for hp in pairs]
        st = {}

        def scores():
            z = [lax.dot_general(qs[hp], k_ref[pl.ds(k0, blk), cols[hp]],
                                 (((1,), (1,)), ((), ())), preferred_element_type=F32)
                 for k0, _, _, hp in chains]
            st["log_beta"], st["keep_cost"], st["row_sum"] = [], [], []
            for (_, mask, _, _), zc in zip(chains, z):
                softplus = jnp.maximum(zc, 0.0) + jnp.log(1.0 + jnp.exp2(jnp.abs(zc) * -LOG2E))
                st["log_beta"].append(zc - softplus)
                if mask is not None:
                    softplus = jnp.where(mask, softplus, 0.0)
                st["keep_cost"].append(softplus.astype(BF16))
                st["row_sum"].append(jnp.broadcast_to(
                    jnp.sum(softplus, axis=1, keepdims=True), (2 * blk, V7X_LANES)))

        def suffixes():
            neg_upper = neg_upper_ref[...]
            suffix = [jnp.dot(kc, neg_upper, preferred_element_type=F32)
                      for kc in st["keep_cost"]]
            carry = list(carry_in())
            st["wgt"] = []
            for c, (_, mask, bias, hp) in enumerate(chains):
                logw = st["log_beta"][c] + suffix[c]
                chunk_carry = carry[hp]
                if bias is not None:
                    chunk_carry = bias if chunk_carry is None else chunk_carry + bias
                if chunk_carry is not None:
                    logw = logw + jnp.concatenate([chunk_carry] * (blk // V7X_LANES), axis=1)
                w = jnp.exp2(logw * LOG2E)
                if mask is not None:
                    w = jnp.where(mask, w, 0.0)
                st["wgt"].append(w.astype(BF16))
                row_sum = st["row_sum"][c]
                carry[hp] = -row_sum if chunk_carry is None else chunk_carry - row_sum
            st["carry"] = carry

        def values():
            out = []
            for hp in pairs:
                acc = None
                for c, (k0, _, _, chain_hp) in enumerate(chains):
                    if chain_hp == hp:
                        pv = jnp.dot(st["wgt"][c], v_ref[pl.ds(k0, blk), cols[hp]],
                                     preferred_element_type=F32)
                        acc = pv if acc is None else acc + pv
                out.append((acc, st["carry"][hp]))
            return out

        return scores, suffixes, values, st

    def mlp_phases(slot):
        st = {}
        d_ff = w1_ref.shape[1]
        chunk = min(MLP_FF_CHUNK, d_ff)

        def out_proj():
            w = MIXER_WIDTH
            h = h_ref[...] + (jnp.dot(c_ref[...], wout_ref[:w, :], preferred_element_type=F32)
                              + jnp.dot(d_ref[slot], wout_ref[w:, :], preferred_element_type=F32))
            st["hn"] = _rmsnorm(h, g_ref[...]).astype(BF16)
            st["acc"] = h

        def up(c):
            def run():
                a = jnp.dot(st["hn"], w1_ref[:, c * chunk:(c + 1) * chunk],
                            preferred_element_type=F32)
                r = jnp.maximum(a, 0.0)
                st["r"] = (r * r).astype(BF16)
            return run

        def down(c):
            def run():
                st["acc"] = st["acc"] + jnp.dot(st["r"], w2_ref[c * chunk:(c + 1) * chunk, :],
                                                preferred_element_type=F32)
            return run

        def finish():
            o_ref[...] = _rmsnorm(st["acc"], fg_ref[...])

        steps = [out_proj]
        for c in range(d_ff // chunk):
            steps += [up(c), down(c)]
        return steps, finish

    def largest_carry():
        top = carry_ref[0]
        for hp in range(1, n_pairs):
            top = jnp.maximum(top, carry_ref[hp])
        return jnp.max(top)

    def run_step(with_attention, with_mlp):
        slot = lax.rem(s, 2)
        if with_mlp:
            mlp_steps, mlp_finish = mlp_phases(1 - slot)
        else:
            mlp_steps, mlp_finish = [], None
        if with_attention:
            i = lax.rem(s, T // blk)
            q0 = pl.multiple_of(i * blk, blk)
            qs = []
            for hp in pairs:
                q2 = q_ref[:, cols[hp]]
                zero = jnp.zeros_like(q2)
                qs.append(jnp.concatenate(
                    [jnp.where(first_head, q2, zero), jnp.where(first_head, zero, q2)], axis=0))
            prev0 = pl.multiple_of(jnp.maximum(q0 - blk, 0), blk)
            prev_bias = jnp.where(i == 0, CHUNK_OFF, 0.0).astype(F32)
            prev_bias = jnp.broadcast_to(prev_bias, (2 * blk, V7X_LANES))
            d_scores, d_suffixes, d_values, d_state = attention_phases(
                qs, [(q0, causal, None)], lambda: [None] * n_pairs)
            p_scores, p_suffixes, p_values, _ = attention_phases(
                qs, [(prev0, None, prev_bias)], lambda: [-rs for rs in d_state["row_sum"]])
            mlp_steps = list(mlp_steps)

            def mlp(count):
                for _ in range(min(count, len(mlp_steps))):
                    mlp_steps.pop(0)()

            d_scores()
            p_scores()
            mlp(3)
            d_suffixes()
            mlp(1)
            p_suffixes()
            mlp(2)
            d_out = d_values()
            mlp(1)
            for hp, ((d_acc, _), (p_acc, carry)) in enumerate(zip(d_out, p_values())):
                acc_ref[hp] = d_acc + p_acc
                carry_ref[hp] = carry
            mlp(len(mlp_steps))
        else:
            for step in mlp_steps:
                step()
        if with_mlp:
            mlp_finish()
        if with_attention:
            def more_to_do(state):
                t, top = state
                return jnp.logical_and(t < i - 1, top > WEIGHT_UNDERFLOW_LOG)

            def farther(state):
                t, _ = state
                k0 = pl.multiple_of((i - 2 - t) * blk, blk)
                sc, su, va, _ = attention_phases(qs, [(k0, None, None)],
                                                 lambda: [carry_ref[hp] for hp in pairs])
                sc()
                su()
                for hp, (pv, carry) in enumerate(va()):
                    acc_ref[hp] += pv
                    carry_ref[hp] = carry
                return t + 1, largest_carry()

            lax.while_loop(more_to_do, farther, (jnp.int32(0), largest_carry()))
            for hp in pairs:
                acc = acc_ref[hp]
                d_ref[slot, :, cols[hp]] = jnp.where(
                    first_head, acc[:blk], acc[blk:]).astype(d_ref.dtype)

    @pl.when(s == 0)
    def _():
        run_step(True, False)

    @pl.when(jnp.logical_and(s > 0, s < n_blocks))
    def _():
        run_step(True, True)

    @pl.when(s == n_blocks)
    def _():
        run_step(False, True)


def _attn_mlp_final(q, k, v, h, c, w_out, g, w1_layers, w2_layers, layer, final_g):
    B, T, W = q.shape
    N, D = h.shape
    blk = ATT_BLOCK
    per_seq = T // blk
    n_blocks = N // blk
    neg_upper = -jnp.triu(jnp.ones((blk, blk), BF16), 1).T
    n_pairs = W // V7X_LANES

    def attn_block(s):
        return jnp.minimum(s, n_blocks - 1)

    def mlp_block(s):
        return jnp.maximum(s - 1, 0)

    q_spec = pl.BlockSpec((None, blk, W),
                          lambda s: (attn_block(s) // per_seq, attn_block(s) % per_seq, 0))
    kv_spec = pl.BlockSpec((None, T, W), lambda s: (attn_block(s) // per_seq, 0, 0))
    row_spec = pl.BlockSpec((blk, D), lambda s: (mlp_block(s), 0))
    half_spec = pl.BlockSpec((blk, W), lambda s: (mlp_block(s), 0))
    return pl.pallas_call(
        _attn_mlp_kernel,
        out_shape=jax.ShapeDtypeStruct((N, D), F32),
        grid=(n_blocks + 1,),
        in_specs=[q_spec, kv_spec, kv_spec, _resident(neg_upper.shape), row_spec, half_spec,
                  _resident(w_out.shape), _resident(g.shape),
                  _resident_layer(w1_layers.shape, layer),
                  _resident_layer(w2_layers.shape, layer), _resident(final_g.shape)],
        out_specs=row_spec,
        scratch_shapes=[pltpu.VMEM((n_pairs, 2 * blk, V7X_LANES), F32),
                        pltpu.VMEM((n_pairs, 2 * blk, V7X_LANES), F32),
                        pltpu.VMEM((2, blk, W), BF16)],
        compiler_params=pltpu.CompilerParams(
            dimension_semantics=("arbitrary",), vmem_limit_bytes=VMEM_LIMIT_BYTES),
        name="attention_out_mlp1_final",
    )(q, k, v, neg_upper, h, c, w_out, g, w1_layers, w2_layers, final_g)


def kernel(x, mix_norm_g, mlp_norm_g, ab_w_in, pool_w, pool_scale, conv_w, conv_b, ab_w_out,
           cd_w_in, sgu_norm_g, sgu_norm_b, sgu_w, sgu_b, cd_w_out, mlp_w1, mlp_w2, final_norm_g):
    B, T, D = x.shape
    N = B * T
    row = lambda vec: vec.reshape(1, -1)

    h = _l0_mixer(x, row(mix_norm_g[0]), ab_w_in[0].astype(BF16), pool_w[0], pool_scale[0],
                  conv_w[0], row(conv_b[0]), ab_w_out[0])
    w1_layers, w2_layers = mlp_w1.astype(BF16), mlp_w2.astype(BF16)
    h = _mlp(h.reshape(N, D), row(mlp_norm_g[0]), w1_layers, w2_layers, 0)

    b_s = jnp.broadcast_to(sgu_b[0][:, :, None], sgu_b[0].shape + (GROUP_WIDTH,))
    c, q, k, v = _l1_front(h, row(mix_norm_g[1]), cd_w_in[0].astype(BF16), row(sgu_norm_g[0]),
                           row(sgu_norm_b[0]), sgu_w[0], b_s)
    shape3 = (B, T, MIXER_WIDTH)
    out = _attn_mlp_final(q.reshape(shape3), k.reshape(shape3), v.reshape(shape3), h, c,
                          cd_w_out[0].astype(BF16), row(mlp_norm_g[1]), w1_layers, w2_layers, 1,
                          row(final_norm_g))
    return out.reshape(B, T, D)
```

```python
import math

import jax
import jax.numpy as jnp
from jax import lax
from jax.experimental import pallas as pl
from jax.experimental.pallas import tpu as pltpu

F32 = jnp.float32
BF16 = jnp.bfloat16

EPS = 1e-6
POOL_WINDOWS = (2, 4, 8, 16)
GROUP_WIDTH = 128
MIXER_WIDTH = 512
SGU_CHUNK = 128
SB_DH = 64

V7X_LANES = 128
V7X_SUBLANES = 8
V7X_VMEM_BYTES = 64 * 1024 * 1024
VMEM_LIMIT_BYTES = V7X_VMEM_BYTES - 8 * 1024 * 1024

ROW_TILE = 1024
POOL_HALO = 16
CONV_HALO = V7X_SUBLANES
ATT_BLOCK = 256
MLP_FF_CHUNK = 1024

LOG2E = math.log2(math.e)
WEIGHT_UNDERFLOW_LOG = -104.0
CHUNK_OFF = -1e30


def _resident(shape):
    zeros = (0,) * len(shape)
    return pl.BlockSpec(shape, lambda *_: zeros, pipeline_mode=pl.Buffered(1))


def _resident_layer(stacked_shape, layer):
    zeros = (0,) * (len(stacked_shape) - 1)
    return pl.BlockSpec((None,) + tuple(stacked_shape[1:]), lambda *_: (layer,) + zeros,
                        pipeline_mode=pl.Buffered(1))


def _rmsnorm(x, g):
    ms = jnp.mean(x * x, axis=-1, keepdims=True)
    return x * lax.rsqrt(ms + EPS) * g


def _trailing_window_sum(e, window):
    s = e
    span = 1
    while span < window:
        s = s + pltpu.roll(s, span, axis=0)
        span *= 2
    return s


def _l0_mixer_kernel(x_ref, g_ref, win_ref, poolw_ref, pscale_ref, convw_ref, convb_ref,
                     wout_ref, o_ref, a_carry, u_carry, wmix_ref, win_bf16):
    j = pl.program_id(1)
    tm = x_ref.shape[0]
    w = MIXER_WIDTH

    @pl.when(jnp.logical_and(pl.program_id(0) == 0, j == 0))
    def _():
        win_bf16[...] = win_ref[...].astype(BF16)
        for grp in range(len(POOL_WINDOWS)):
            rows = slice(grp * GROUP_WIDTH, (grp + 1) * GROUP_WIDTH)
            scaled = poolw_ref[grp] * pscale_ref[grp:grp + 1, :]
            wmix_ref[rows, :] = jnp.dot(scaled, wout_ref[rows, :], preferred_element_type=F32,
                                        precision=lax.Precision.HIGHEST).astype(BF16)
        wmix_ref[w:, :] = wout_ref[w:, :].astype(BF16)

    @pl.when(j == 0)
    def _():
        a_carry[...] = jnp.zeros_like(a_carry)
        u_carry[...] = jnp.zeros_like(u_carry)

    x = x_ref[...]
    xn = _rmsnorm(x, g_ref[...]).astype(BF16)
    p = jnp.dot(xn, win_bf16[...], preferred_element_type=F32)
    a, xb, gate_b, gate_c = p[:, :w], p[:, w:2 * w], p[:, 2 * w:3 * w], p[:, 3 * w:]

    a_ext = jnp.concatenate([a_carry[...], a], axis=0)
    a_carry[...] = a[tm - POOL_HALO:, :]
    pos = j * tm + lax.broadcasted_iota(jnp.int32, (tm, 1), 0)
    outs = []
    for grp, window in enumerate(POOL_WINDOWS):
        cols = slice(grp * GROUP_WIDTH, (grp + 1) * GROUP_WIDTH)
        wsum = _trailing_window_sum(a_ext[:, cols], window)[POOL_HALO:, :]
        count = jnp.minimum(pos + 1, window).astype(F32)
        outs.append(wsum / count - a[:, cols])

    u = gate_c * xb
    u_ext = jnp.concatenate([u_carry[...], u], axis=0)
    u_carry[...] = u[tm - CONV_HALO:, :]
    u1 = pltpu.roll(u_ext, 1, axis=0)[CONV_HALO:, :]
    u2 = pltpu.roll(u_ext, 2, axis=0)[CONV_HALO:, :]
    y = u2 * convw_ref[0:1, :] + u1 * convw_ref[1:2, :] + u * convw_ref[2:3, :] + convb_ref[...]
    outs.append(gate_b * y)

    mix_in = jnp.concatenate(outs, axis=1).astype(BF16)
    o_ref[...] = x + jnp.dot(mix_in, wmix_ref[...], preferred_element_type=F32)


def _l0_mixer(x, g, w_in, pool_w, pool_scale, conv_w, conv_b, w_out):
    B, T, D = x.shape
    tm = min(ROW_TILE, T)
    assert T % tm == 0 and tm >= POOL_HALO, (T, tm)
    grid = (B, T // tm)
    row_spec = pl.BlockSpec((None, tm, D), lambda b, j: (b, j, 0))
    return pl.pallas_call(
        _l0_mixer_kernel,
        out_shape=jax.ShapeDtypeStruct((B, T, D), F32),
        grid=grid,
        in_specs=[row_spec, _resident(g.shape), _resident(w_in.shape), _resident(pool_w.shape),
                  _resident(pool_scale.shape), _resident(conv_w.shape), _resident(conv_b.shape),
                  _resident(w_out.shape)],
        out_specs=row_spec,
        scratch_shapes=[pltpu.VMEM((POOL_HALO, MIXER_WIDTH), F32),
                        pltpu.VMEM((CONV_HALO, MIXER_WIDTH), F32),
                        pltpu.VMEM(w_out.shape, BF16),
                        pltpu.VMEM(w_in.shape, BF16)],
        compiler_params=pltpu.CompilerParams(
            dimension_semantics=("arbitrary", "arbitrary"), vmem_limit_bytes=VMEM_LIMIT_BYTES),
        name="l0_mixer",
    )(x, g, w_in, pool_w, pool_scale, conv_w, conv_b, w_out)


def _mlp_kernel(h_ref, g_ref, w1_ref, w2_ref, o_ref):
    h = h_ref[...]
    d_ff = w1_ref.shape[1]
    chunk = min(MLP_FF_CHUNK, d_ff)
    hn = _rmsnorm(h, g_ref[...]).astype(BF16)
    acc = h
    for c in range(d_ff // chunk):
        cols = slice(c * chunk, (c + 1) * chunk)
        a = jnp.dot(hn, w1_ref[:, cols], preferred_element_type=F32)
        r = jnp.maximum(a, 0.0)
        acc = acc + jnp.dot((r * r).astype(BF16), w2_ref[cols, :], preferred_element_type=F32)
    o_ref[...] = acc


def _mlp(h, g, w1_layers, w2_layers, layer):
    N, D = h.shape
    tm = min(ROW_TILE, N)
    assert N % tm == 0, (N, tm)
    row_spec = pl.BlockSpec((tm, D), lambda i: (i, 0))
    return pl.pallas_call(
        _mlp_kernel,
        out_shape=jax.ShapeDtypeStruct((N, D), F32),
        grid=(N // tm,),
        in_specs=[row_spec, _resident(g.shape), _resident_layer(w1_layers.shape, layer),
                  _resident_layer(w2_layers.shape, layer)],
        out_specs=row_spec,
        compiler_params=pltpu.CompilerParams(
            dimension_semantics=("arbitrary",), vmem_limit_bytes=VMEM_LIMIT_BYTES),
        name="mlp0",
    )(h, g, w1_layers, w2_layers)


def _gelu(x):
    return 0.5 * x * (1.0 + lax.erf(x * (1.0 / math.sqrt(2.0))))


def _l1_front_kernel(h_ref, g_ref, win_ref, lng_ref, lnb_ref, ws_ref, bs_ref,
                     c_ref, q_ref, k_ref, v_ref, win_bf16):
    tm = h_ref.shape[0]
    w = MIXER_WIDTH

    @pl.when(pl.program_id(0) == 0)
    def _():
        win_bf16[...] = win_ref[...].astype(BF16)

    xn = _rmsnorm(h_ref[...], g_ref[...]).astype(BF16)
    p = jnp.dot(xn, win_bf16[...], preferred_element_type=F32)
    u = _gelu(p[:, :w])
    v = _gelu(p[:, w:2 * w])
    q_ref[...] = (p[:, 2 * w:3 * w] * (1.0 / math.sqrt(SB_DH))).astype(BF16)
    k_ref[...] = p[:, 3 * w:4 * w].astype(BF16)
    v_ref[...] = p[:, 4 * w:].astype(BF16)

    mu = jnp.mean(v, axis=-1, keepdims=True)
    vc = v - mu
    var = jnp.mean(vc * vc, axis=-1, keepdims=True)
    vn = (vc * lax.rsqrt(var + EPS) * lng_ref[...] + lnb_ref[...]).astype(BF16)

    L = SGU_CHUNK
    tpos = lax.broadcasted_iota(jnp.int32, (L, L), 0)
    spos = lax.broadcasted_iota(jnp.int32, (L, L), 1)
    causal = spos <= tpos
    for grp in range(w // GROUP_WIDTH):
        cols = slice(grp * GROUP_WIDTH, (grp + 1) * GROUP_WIDTH)
        w_s = jnp.where(causal, ws_ref[grp], 0.0).astype(BF16)
        bias = bs_ref[grp]
        for n in range(tm // L):
            rows = slice(n * L, (n + 1) * L)
            s = jnp.dot(w_s, vn[rows, cols], preferred_element_type=F32) + bias
            c_ref[rows, cols] = (u[rows, cols] * s).astype(BF16)


def _l1_front(h, g, w_in, ln_g, ln_b, w_s, b_s):
    N, D = h.shape
    tm = min(ROW_TILE, N)
    assert N % tm == 0 and tm % SGU_CHUNK == 0, (N, tm)
    row_spec = pl.BlockSpec((tm, D), lambda i: (i, 0))
    half_spec = pl.BlockSpec((tm, MIXER_WIDTH), lambda i: (i, 0))
    half_shape = jax.ShapeDtypeStruct((N, MIXER_WIDTH), BF16)
    return pl.pallas_call(
        _l1_front_kernel,
        out_shape=(half_shape,) * 4,
        grid=(N // tm,),
        in_specs=[row_spec, _resident(g.shape), _resident(w_in.shape), _resident(ln_g.shape),
                  _resident(ln_b.shape), _resident(w_s.shape), _resident(b_s.shape)],
        out_specs=(half_spec,) * 4,
        scratch_shapes=[pltpu.VMEM(w_in.shape, BF16)],
        compiler_params=pltpu.CompilerParams(
            dimension_semantics=("arbitrary",), vmem_limit_bytes=VMEM_LIMIT_BYTES),
        name="l1_front",
    )(h, g, w_in, ln_g, ln_b, w_s, b_s)


def _attn_mlp_kernel(q_ref, k_ref, v_ref, neg_upper_ref, h_ref, c_ref, wout_ref, g_ref, w1_ref,
                     w2_ref, fg_ref, o_ref, acc_ref, carry_ref, d_ref):
    s = pl.program_id(0)
    n_blocks = pl.num_programs(0) - 1
    blk = ATT_BLOCK
    T, W = k_ref.shape
    n_pairs = W // V7X_LANES
    pairs = range(n_pairs)
    cols = [slice(hp * V7X_LANES, (hp + 1) * V7X_LANES) for hp in pairs]
    lane = lax.broadcasted_iota(jnp.int32, (blk, V7X_LANES), 1)
    first_head = lane < SB_DH
    row_q = lax.broadcasted_iota(jnp.int32, (2 * blk, 1), 0) & (blk - 1)
    col_k = lax.broadcasted_iota(jnp.int32, (1, blk), 1)
    causal = col_k < row_q

    def attention_phases(qs, chunks, carry_in, hps=None):
        hps = list(pairs) if hps is None else hps
        chains = [(k0, mask, bias, hp) for k0, mask, bias in chunks for hp in hps]
        st = {}

        def scores():
            z = [lax.dot_general(qs[hp], k_ref[pl.ds(k0, blk), cols[hp]],
                                 (((1,), (1,)), ((), ())), preferred_element_type=F32)
                 for k0, _, _, hp in chains]
            st["log_beta"], st["keep_cost"], st["row_sum"] = [], [], []
            for (_, mask, _, _), zc in zip(chains, z):
                softplus = jnp.maximum(zc, 0.0) + jnp.log(1.0 + jnp.exp2(jnp.abs(zc) * -LOG2E))
                st["log_beta"].append(zc - softplus)
                if mask is not None:
                    softplus = jnp.where(mask, softplus, 0.0)
                st["keep_cost"].append(softplus.astype(BF16))
                st["row_sum"].append(jnp.broadcast_to(
                    jnp.sum(softplus, axis=1, keepdims=True), (2 * blk, V7X_LANES)))

        def suffixes():
            neg_upper = neg_upper_ref[...]
            suffix = [jnp.dot(kc, neg_upper, preferred_element_type=F32)
                      for kc in st["keep_cost"]]
            carry = list(carry_in())
            st["wgt"] = []
            for c, (_, mask, bias, hp) in enumerate(chains):
                logw = st["log_beta"][c] + suffix[c]
                chunk_carry = carry[hp]
                if bias is not None:
                    chunk_carry = bias if chunk_carry is None else chunk_carry + bias
                if chunk_carry is not None:
                    logw = logw + jnp.concatenate([chunk_carry] * (blk // V7X_LANES), axis=1)
                w = jnp.exp2(logw * LOG2E)
                if mask is not None:
                    w = jnp.where(mask, w, 0.0)
                st["wgt"].append(w.astype(BF16))
                row_sum = st["row_sum"][c]
                carry[hp] = -row_sum if chunk_carry is None else chunk_carry - row_sum
            st["carry"] = carry

        def values():
            out = []
            for hp in hps:
                acc = None
                for c, (k0, _, _, chain_hp) in enumerate(chains):
                    if chain_hp == hp:
                        pv = jnp.dot(st["wgt"][c], v_ref[pl.ds(k0, blk), cols[hp]],
                                     preferred_element_type=F32)
                        acc = pv if acc is None else acc + pv
                out.append((acc, st["carry"][hp]))
            return out

        return scores, suffixes, values, st

    def mlp_phases(slot):
        st = {}
        d_ff = w1_ref.shape[1]
        chunk = min(MLP_FF_CHUNK, d_ff)

        def out_proj():
            w = MIXER_WIDTH
            h = h_ref[...] + (jnp.dot(c_ref[...], wout_ref[:w, :], preferred_element_type=F32)
                              + jnp.dot(d_ref[slot], wout_ref[w:, :], preferred_element_type=F32))
            st["hn"] = _rmsnorm(h, g_ref[...]).astype(BF16)
            st["acc"] = h

        def up(c):
            def run():
                a = jnp.dot(st["hn"], w1_ref[:, c * chunk:(c + 1) * chunk],
                            preferred_element_type=F32)
                r = jnp.maximum(a, 0.0)
                st["r"] = (r * r).astype(BF16)
            return run

        def down(c):
            def run():
                st["acc"] = st["acc"] + jnp.dot(st["r"], w2_ref[c * chunk:(c + 1) * chunk, :],
                                                preferred_element_type=F32)
            return run

        def finish():
            o_ref[...] = _rmsnorm(st["acc"], fg_ref[...])

        steps = [out_proj]
        for c in range(d_ff // chunk):
            steps += [up(c), down(c)]
        return steps, finish

    def largest_carry():
        top = carry_ref[0]
        for hp in range(1, n_pairs):
            top = jnp.maximum(top, carry_ref[hp])
        return jnp.max(top)

    def run_step(with_attention, with_mlp):
        slot = lax.rem(s, 2)
        if with_mlp:
            mlp_steps, mlp_finish = mlp_phases(1 - slot)
        else:
            mlp_steps, mlp_finish = [], None
        if with_attention:
            i = lax.rem(s, T // blk)
            q0 = pl.multiple_of(i * blk, blk)
            qs = []
            for hp in pairs:
                q2 = q_ref[:, cols[hp]]
                zero = jnp.zeros_like(q2)
                qs.append(jnp.concatenate(
                    [jnp.where(first_head, q2, zero), jnp.where(first_head, zero, q2)], axis=0))
            prev0 = pl.multiple_of(jnp.maximum(q0 - blk, 0), blk)
            prev_bias = jnp.where(i == 0, CHUNK_OFF, 0.0).astype(F32)
            prev_bias = jnp.broadcast_to(prev_bias, (2 * blk, V7X_LANES))
            groups = [list(pairs)[:n_pairs // 2], list(pairs)[n_pairs // 2:]]
            diag_sets = [attention_phases(qs, [(q0, causal, None)], lambda: [None] * n_pairs, g)
                         for g in groups]

            def carry_after_diag(state, group):
                out = [None] * n_pairs
                for idx, hp in enumerate(group):
                    out[hp] = -state["row_sum"][idx]
                return out

            prev_sets = [attention_phases(qs, [(prev0, None, prev_bias)],
                                          lambda st=ds[3], g=g: carry_after_diag(st, g), g)
                         for ds, g in zip(diag_sets, groups)]
            mlp_steps = list(mlp_steps)

            def mlp(count):
                for _ in range(min(count, len(mlp_steps))):
                    mlp_steps.pop(0)()

            for scores, _, _, _ in diag_sets + prev_sets:
                scores()
            mlp(2)
            for _, suffixes, _, _ in diag_sets + prev_sets:
                suffixes()
                mlp(1)
            diag_out = [values() for _, _, values, _ in diag_sets]
            mlp(1)
            for d_out, (_, _, values, _), group in zip(diag_out, prev_sets, groups):
                for (d_acc, _), (p_acc, carry), hp in zip(d_out, values(), group):
                    acc_ref[hp] = d_acc + p_acc
                    carry_ref[hp] = carry
            mlp(len(mlp_steps))
        else:
            for step in mlp_steps:
                step()
        if with_mlp:
            mlp_finish()
        if with_attention:
            def more_to_do(state):
                t, top = state
                return jnp.logical_and(t < i - 1, top > WEIGHT_UNDERFLOW_LOG)

            def farther(state):
                t, _ = state
                k0 = pl.multiple_of((i - 2 - t) * blk, blk)
                sc, su, va, _ = attention_phases(qs, [(k0, None, None)],
                                                 lambda: [carry_ref[hp] for hp in pairs])
                sc()
                su()
                for hp, (pv, carry) in enumerate(va()):
                    acc_ref[hp] += pv
                    carry_ref[hp] = carry
                return t + 1, largest_carry()

            lax.while_loop(more_to_do, farther, (jnp.int32(0), largest_carry()))
            for hp in pairs:
                acc = acc_ref[hp]
                d_ref[slot, :, cols[hp]] = jnp.where(
                    first_head, acc[:blk], acc[blk:]).astype(d_ref.dtype)

    @pl.when(s == 0)
    def _():
        run_step(True, False)

    @pl.when(jnp.logical_and(s > 0, s < n_blocks))
    def _():
        run_step(True, True)

    @pl.when(s == n_blocks)
    def _():
        run_step(False, True)


def _attn_mlp_final(q, k, v, h, c, w_out, g, w1_layers, w2_layers, layer, final_g):
    B, T, W = q.shape
    N, D = h.shape
    blk = ATT_BLOCK
    assert T % blk == 0 and N == B * T and W % V7X_LANES == 0, (B, T, W, N)
    per_seq = T // blk
    n_blocks = N // blk
    neg_upper = -jnp.triu(jnp.ones((blk, blk), BF16), 1).T
    n_pairs = W // V7X_LANES

    def attn_block(s):
        return jnp.minimum(s, n_blocks - 1)

    def mlp_block(s):
        return jnp.maximum(s - 1, 0)

    q_spec = pl.BlockSpec((None, blk, W),
                          lambda s: (attn_block(s) // per_seq, attn_block(s) % per_seq, 0))
    kv_spec = pl.BlockSpec((None, T, W), lambda s: (attn_block(s) // per_seq, 0, 0))
    row_spec = pl.BlockSpec((blk, D), lambda s: (mlp_block(s), 0))
    half_spec = pl.BlockSpec((blk, W), lambda s: (mlp_block(s), 0))
    return pl.pallas_call(
        _attn_mlp_kernel,
        out_shape=jax.ShapeDtypeStruct((N, D), F32),
        grid=(n_blocks + 1,),
        in_specs=[q_spec, kv_spec, kv_spec, _resident(neg_upper.shape), row_spec, half_spec,
                  _resident(w_out.shape), _resident(g.shape),
                  _resident_layer(w1_layers.shape, layer),
                  _resident_layer(w2_layers.shape, layer), _resident(final_g.shape)],
        out_specs=row_spec,
        scratch_shapes=[pltpu.VMEM((n_pairs, 2 * blk, V7X_LANES), F32),
                        pltpu.VMEM((n_pairs, 2 * blk, V7X_LANES), F32),
                        pltpu.VMEM((2, blk, W), BF16)],
        compiler_params=pltpu.CompilerParams(
            dimension_semantics=("arbitrary",), vmem_limit_bytes=VMEM_LIMIT_BYTES),
        name="attention_out_mlp1_final",
    )(q, k, v, neg_upper, h, c, w_out, g, w1_layers, w2_layers, final_g)


def kernel(x, mix_norm_g, mlp_norm_g, ab_w_in, pool_w, pool_scale, conv_w, conv_b, ab_w_out,
           cd_w_in, sgu_norm_g, sgu_norm_b, sgu_w, sgu_b, cd_w_out, mlp_w1, mlp_w2, final_norm_g):
    B, T, D = x.shape
    N = B * T
    row = lambda vec: vec.reshape(1, -1)

    h = _l0_mixer(x, row(mix_norm_g[0]), ab_w_in[0], pool_w[0], pool_scale[0],
                  conv_w[0], row(conv_b[0]), ab_w_out[0])
    w1_layers, w2_layers = mlp_w1.astype(BF16), mlp_w2.astype(BF16)
    h = _mlp(h.reshape(N, D), row(mlp_norm_g[0]), w1_layers, w2_layers, 0)

    b_s = jnp.broadcast_to(sgu_b[0][:, :, None], sgu_b[0].shape + (GROUP_WIDTH,))
    c, q, k, v = _l1_front(h, row(mix_norm_g[1]), cd_w_in[0], row(sgu_norm_g[0]),
                           row(sgu_norm_b[0]), sgu_w[0], b_s)
    shape3 = (B, T, MIXER_WIDTH)
    out = _attn_mlp_final(q.reshape(shape3), k.reshape(shape3), v.reshape(shape3), h, c,
                          cd_w_out[0].astype(BF16), row(mlp_norm_g[1]), w1_layers, w2_layers, 1,
                          row(final_norm_g))
    return out.reshape(B, T, D)
```

```python
import math

import jax
import jax.numpy as jnp
from jax import lax
from jax.experimental import pallas as pl
from jax.experimental.pallas import tpu as pltpu

F32 = jnp.float32
BF16 = jnp.bfloat16

EPS = 1e-6
POOL_WINDOWS = (2, 4, 8, 16)
GROUP_WIDTH = 128
MIXER_WIDTH = 512
SGU_CHUNK = 128
SB_DH = 64

V7X_LANES = 128
V7X_SUBLANES = 8
V7X_VMEM_BYTES = 64 * 1024 * 1024
VMEM_LIMIT_BYTES = V7X_VMEM_BYTES - 8 * 1024 * 1024

ROW_TILE = 1024
POOL_HALO = 16
CONV_HALO = V7X_SUBLANES
ATT_BLOCK = 256
MLP_FF_CHUNK = 1024

LOG2E = math.log2(math.e)
WEIGHT_UNDERFLOW_LOG = -104.0
CHUNK_OFF = -1e30


def _resident(shape):
    zeros = (0,) * len(shape)
    return pl.BlockSpec(shape, lambda *_: zeros, pipeline_mode=pl.Buffered(1))


def _resident_layer(stacked_shape, layer):
    zeros = (0,) * (len(stacked_shape) - 1)
    return pl.BlockSpec((None,) + tuple(stacked_shape[1:]), lambda *_: (layer,) + zeros,
                        pipeline_mode=pl.Buffered(1))


def _rmsnorm(x, g):
    ms = jnp.mean(x * x, axis=-1, keepdims=True)
    return x * lax.rsqrt(ms + EPS) * g


def _trailing_window_sum(e, window):
    s = e
    span = 1
    while span < window:
        s = s + pltpu.roll(s, span, axis=0)
        span *= 2
    return s


def _l0_mixer_kernel(x_ref, g_ref, win_ref, poolw_ref, pscale_ref, convw_ref, convb_ref,
                     wout_ref, o_ref, a_carry, u_carry, wmix_ref, win_bf16):
    j = pl.program_id(1)
    tm = x_ref.shape[0]
    w = MIXER_WIDTH

    @pl.when(jnp.logical_and(pl.program_id(0) == 0, j == 0))
    def _():
        win_bf16[...] = win_ref[...].astype(BF16)
        for grp in range(len(POOL_WINDOWS)):
            rows = slice(grp * GROUP_WIDTH, (grp + 1) * GROUP_WIDTH)
            scaled = poolw_ref[grp] * pscale_ref[grp:grp + 1, :]
            wmix_ref[rows, :] = jnp.dot(scaled, wout_ref[rows, :], preferred_element_type=F32,
                                        precision=lax.Precision.HIGHEST).astype(BF16)
        wmix_ref[w:, :] = wout_ref[w:, :].astype(BF16)

    @pl.when(j == 0)
    def _():
        a_carry[...] = jnp.zeros_like(a_carry)
        u_carry[...] = jnp.zeros_like(u_carry)

    x = x_ref[...]
    xn = _rmsnorm(x, g_ref[...]).astype(BF16)
    p = jnp.dot(xn, win_bf16[...], preferred_element_type=F32)
    a, xb, gate_b, gate_c = p[:, :w], p[:, w:2 * w], p[:, 2 * w:3 * w], p[:, 3 * w:]

    a_ext = jnp.concatenate([a_carry[...], a], axis=0)
    a_carry[...] = a[tm - POOL_HALO:, :]
    pos = j * tm + lax.broadcasted_iota(jnp.int32, (tm, 1), 0)
    outs = []
    for grp, window in enumerate(POOL_WINDOWS):
        cols = slice(grp * GROUP_WIDTH, (grp + 1) * GROUP_WIDTH)
        wsum = _trailing_window_sum(a_ext[:, cols], window)[POOL_HALO:, :]
        count = jnp.minimum(pos + 1, window).astype(F32)
        outs.append(wsum / count - a[:, cols])

    u = gate_c * xb
    u_ext = jnp.concatenate([u_carry[...], u], axis=0)
    u_carry[...] = u[tm - CONV_HALO:, :]
    u1 = pltpu.roll(u_ext, 1, axis=0)[CONV_HALO:, :]
    u2 = pltpu.roll(u_ext, 2, axis=0)[CONV_HALO:, :]
    y = u2 * convw_ref[0:1, :] + u1 * convw_ref[1:2, :] + u * convw_ref[2:3, :] + convb_ref[...]
    outs.append(gate_b * y)

    mix_in = jnp.concatenate(outs, axis=1).astype(BF16)
    o_ref[...] = x + jnp.dot(mix_in, wmix_ref[...], preferred_element_type=F32)


def _l0_mixer(x, g, w_in, pool_w, pool_scale, conv_w, conv_b, w_out):
    B, T, D = x.shape
    tm = min(ROW_TILE, T)
    assert T % tm == 0 and tm >= POOL_HALO, (T, tm)
    grid = (B, T // tm)
    row_spec = pl.BlockSpec((None, tm, D), lambda b, j: (b, j, 0))
    return pl.pallas_call(
        _l0_mixer_kernel,
        out_shape=jax.ShapeDtypeStruct((B, T, D), F32),
        grid=grid,
        in_specs=[row_spec, _resident(g.shape), _resident(w_in.shape), _resident(pool_w.shape),
                  _resident(pool_scale.shape), _resident(conv_w.shape), _resident(conv_b.shape),
                  _resident(w_out.shape)],
        out_specs=row_spec,
        scratch_shapes=[pltpu.VMEM((POOL_HALO, MIXER_WIDTH), F32),
                        pltpu.VMEM((CONV_HALO, MIXER_WIDTH), F32),
                        pltpu.VMEM(w_out.shape, BF16),
                        pltpu.VMEM(w_in.shape, BF16)],
        compiler_params=pltpu.CompilerParams(
            dimension_semantics=("arbitrary", "arbitrary"), vmem_limit_bytes=VMEM_LIMIT_BYTES),
        name="l0_mixer",
    )(x, g, w_in, pool_w, pool_scale, conv_w, conv_b, w_out)


def _mlp_kernel(h_ref, g_ref, w1_ref, w2_ref, o_ref):
    h = h_ref[...]
    d_ff = w1_ref.shape[1]
    chunk = min(MLP_FF_CHUNK, d_ff)
    hn = _rmsnorm(h, g_ref[...]).astype(BF16)
    acc = h
    for c in range(d_ff // chunk):
        cols = slice(c * chunk, (c + 1) * chunk)
        a = jnp.dot(hn, w1_ref[:, cols], preferred_element_type=F32)
        r = jnp.maximum(a, 0.0)
        acc = acc + jnp.dot((r * r).astype(BF16), w2_ref[cols, :], preferred_element_type=F32)
    o_ref[...] = acc


def _mlp(h, g, w1_layers, w2_layers, layer):
    N, D = h.shape
    tm = min(ROW_TILE, N)
    assert N % tm == 0, (N, tm)
    row_spec = pl.BlockSpec((tm, D), lambda i: (i, 0))
    return pl.pallas_call(
        _mlp_kernel,
        out_shape=jax.ShapeDtypeStruct((N, D), F32),
        grid=(N // tm,),
        in_specs=[row_spec, _resident(g.shape), _resident_layer(w1_layers.shape, layer),
                  _resident_layer(w2_layers.shape, layer)],
        out_specs=row_spec,
        compiler_params=pltpu.CompilerParams(
            dimension_semantics=("arbitrary",), vmem_limit_bytes=VMEM_LIMIT_BYTES),
        name="mlp0",
    )(h, g, w1_layers, w2_layers)


def _gelu(x):
    return 0.5 * x * (1.0 + lax.erf(x * (1.0 / math.sqrt(2.0))))


def _l1_front_kernel(h_ref, g_ref, win_ref, lng_ref, lnb_ref, ws_ref, bs_ref,
                     c_ref, q_ref, k_ref, v_ref, win_bf16):
    tm = h_ref.shape[0]
    w = MIXER_WIDTH

    @pl.when(pl.program_id(0) == 0)
    def _():
        win_bf16[...] = win_ref[...].astype(BF16)

    xn = _rmsnorm(h_ref[...], g_ref[...]).astype(BF16)
    p = jnp.dot(xn, win_bf16[...], preferred_element_type=F32)
    u = _gelu(p[:, :w])
    v = _gelu(p[:, w:2 * w])
    q_ref[...] = (p[:, 2 * w:3 * w] * (1.0 / math.sqrt(SB_DH))).astype(BF16)
    k_ref[...] = p[:, 3 * w:4 * w].astype(BF16)
    v_ref[...] = p[:, 4 * w:].astype(BF16)

    mu = jnp.mean(v, axis=-1, keepdims=True)
    vc = v - mu
    var = jnp.mean(vc * vc, axis=-1, keepdims=True)
    vn = (vc * lax.rsqrt(var + EPS) * lng_ref[...] + lnb_ref[...]).astype(BF16)

    L = SGU_CHUNK
    tpos = lax.broadcasted_iota(jnp.int32, (L, L), 0)
    spos = lax.broadcasted_iota(jnp.int32, (L, L), 1)
    causal = spos <= tpos
    for grp in range(w // GROUP_WIDTH):
        cols = slice(grp * GROUP_WIDTH, (grp + 1) * GROUP_WIDTH)
        w_s = jnp.where(causal, ws_ref[grp], 0.0).astype(BF16)
        bias = bs_ref[grp]
        for n in range(tm // L):
            rows = slice(n * L, (n + 1) * L)
            s = jnp.dot(w_s, vn[rows, cols], preferred_element_type=F32) + bias
            c_ref[rows, cols] = (u[rows, cols] * s).astype(BF16)


def _l1_front(h, g, w_in, ln_g, ln_b, w_s, b_s):
    N, D = h.shape
    tm = min(ROW_TILE, N)
    assert N % tm == 0 and tm % SGU_CHUNK == 0, (N, tm)
    row_spec = pl.BlockSpec((tm, D), lambda i: (i, 0))
    half_spec = pl.BlockSpec((tm, MIXER_WIDTH), lambda i: (i, 0))
    half_shape = jax.ShapeDtypeStruct((N, MIXER_WIDTH), BF16)
    return pl.pallas_call(
        _l1_front_kernel,
        out_shape=(half_shape,) * 4,
        grid=(N // tm,),
        in_specs=[row_spec, _resident(g.shape), _resident(w_in.shape), _resident(ln_g.shape),
                  _resident(ln_b.shape), _resident(w_s.shape), _resident(b_s.shape)],
        out_specs=(half_spec,) * 4,
        scratch_shapes=[pltpu.VMEM(w_in.shape, BF16)],
        compiler_params=pltpu.CompilerParams(
            dimension_semantics=("arbitrary",), vmem_limit_bytes=VMEM_LIMIT_BYTES),
        name="l1_front",
    )(h, g, w_in, ln_g, ln_b, w_s, b_s)


def _attn_mlp_kernel(q_ref, k_ref, v_ref, neg_upper_ref, h_ref, c_ref, wout_ref, g_ref, w1_ref,
                     w2_ref, fg_ref, o_ref, acc_ref, carry_ref, d_ref):
    s = pl.program_id(0)
    n_blocks = pl.num_programs(0) - 1
    blk = ATT_BLOCK
    T, W = k_ref.shape
    n_pairs = W // V7X_LANES
    pairs = range(n_pairs)
    cols = [slice(hp * V7X_LANES, (hp + 1) * V7X_LANES) for hp in pairs]
    lane = lax.broadcasted_iota(jnp.int32, (blk, V7X_LANES), 1)
    first_head = lane < SB_DH
    row_q = lax.broadcasted_iota(jnp.int32, (2 * blk, 1), 0) & (blk - 1)
    col_k = lax.broadcasted_iota(jnp.int32, (1, blk), 1)
    causal = col_k < row_q

    def attention_phases(qs, chunks, carry_in, hps=None):
        hps = list(pairs) if hps is None else hps
        chains = [(k0, mask, bias, hp) for k0, mask, bias in chunks for hp in hps]
        st = {}

        def scores():
            z = [lax.dot_general(qs[hp], k_ref[pl.ds(k0, blk), cols[hp]],
                                 (((1,), (1,)), ((), ())), preferred_element_type=F32)
                 for k0, _, _, hp in chains]
            st["log_beta"], st["keep_cost"], st["row_sum"] = [], [], []
            for (_, mask, _, _), zc in zip(chains, z):
                softplus = jnp.maximum(zc, 0.0) + jnp.log(1.0 + jnp.exp2(jnp.abs(zc) * -LOG2E))
                st["log_beta"].append(zc - softplus)
                if mask is not None:
                    softplus = jnp.where(mask, softplus, 0.0)
                st["keep_cost"].append(softplus.astype(BF16))
                st["row_sum"].append(jnp.broadcast_to(
                    jnp.sum(softplus, axis=1, keepdims=True), (2 * blk, V7X_LANES)))

        def suffixes():
            neg_upper = neg_upper_ref[...]
            suffix = [jnp.dot(kc, neg_upper, preferred_element_type=F32)
                      for kc in st["keep_cost"]]
            carry = list(carry_in())
            st["wgt"] = []
            for c, (_, mask, bias, hp) in enumerate(chains):
                logw = st["log_beta"][c] + suffix[c]
                chunk_carry = carry[hp]
                if bias is not None:
                    chunk_carry = bias if chunk_carry is None else chunk_carry + bias
                if chunk_carry is not None:
                    logw = logw + jnp.concatenate([chunk_carry] * (blk // V7X_LANES), axis=1)
                w = jnp.exp2(logw * LOG2E)
                if mask is not None:
                    w = jnp.where(mask, w, 0.0)
                st["wgt"].append(w.astype(BF16))
                row_sum = st["row_sum"][c]
                carry[hp] = -row_sum if chunk_carry is None else chunk_carry - row_sum
            st["carry"] = carry

        def values():
            out = []
            for hp in hps:
                acc = None
                for c, (k0, _, _, chain_hp) in enumerate(chains):
                    if chain_hp == hp:
                        pv = jnp.dot(st["wgt"][c], v_ref[pl.ds(k0, blk), cols[hp]],
                                     preferred_element_type=F32)
                        acc = pv if acc is None else acc + pv
                out.append((acc, st["carry"][hp]))
            return out

        return scores, suffixes, values, st

    def mlp_phases(slot):
        st = {}
        d_ff = w1_ref.shape[1]
        chunk = min(MLP_FF_CHUNK, d_ff)

        def out_proj():
            w = MIXER_WIDTH
            h = h_ref[...] + (jnp.dot(c_ref[...], wout_ref[:w, :], preferred_element_type=F32)
                              + jnp.dot(d_ref[slot], wout_ref[w:, :], preferred_element_type=F32))
            st["hn"] = _rmsnorm(h, g_ref[...]).astype(BF16)
            st["acc"] = h

        def up(c):
            def run():
                a = jnp.dot(st["hn"], w1_ref[:, c * chunk:(c + 1) * chunk],
                            preferred_element_type=F32)
                r = jnp.maximum(a, 0.0)
                st["r"] = (r * r).astype(BF16)
            return run

        def down(c):
            def run():
                st["acc"] = st["acc"] + jnp.dot(st["r"], w2_ref[c * chunk:(c + 1) * chunk, :],
                                                preferred_element_type=F32)
            return run

        def finish():
            o_ref[...] = _rmsnorm(st["acc"], fg_ref[...])

        steps = [out_proj]
        for c in range(d_ff // chunk):
            steps += [up(c), down(c)]
        return steps, finish

    def largest_carry():
        top = carry_ref[0]
        for hp in range(1, n_pairs):
            top = jnp.maximum(top, carry_ref[hp])
        return jnp.max(top)

    def run_step(with_attention, with_mlp):
        slot = lax.rem(s, 2)
        if with_mlp:
            mlp_steps, mlp_finish = mlp_phases(1 - slot)
        else:
            mlp_steps, mlp_finish = [], None
        if with_attention:
            i = lax.rem(s, T // blk)
            q0 = pl.multiple_of(i * blk, blk)
            qs = []
            for hp in pairs:
                q2 = q_ref[:, cols[hp]]
                zero = jnp.zeros_like(q2)
                qs.append(jnp.concatenate(
                    [jnp.where(first_head, q2, zero), jnp.where(first_head, zero, q2)], axis=0))
            prev0 = pl.multiple_of(jnp.maximum(q0 - blk, 0), blk)
            prev_bias = jnp.where(i == 0, CHUNK_OFF, 0.0).astype(F32)
            prev_bias = jnp.broadcast_to(prev_bias, (2 * blk, V7X_LANES))
            groups = [list(pairs)[:n_pairs // 2], list(pairs)[n_pairs // 2:]]
            diag_sets = [attention_phases(qs, [(q0, causal, None)], lambda: [None] * n_pairs, g)
                         for g in groups]

            def carry_after_diag(state, group):
                out = [None] * n_pairs
                for idx, hp in enumerate(group):
                    out[hp] = -state["row_sum"][idx]
                return out

            prev_sets = [attention_phases(qs, [(prev0, None, prev_bias)],
                                          lambda st=ds[3], g=g: carry_after_diag(st, g), g)
                         for ds, g in zip(diag_sets, groups)]
            mlp_steps = list(mlp_steps)

            def mlp(count):
                for _ in range(min(count, len(mlp_steps))):
                    mlp_steps.pop(0)()

            for scores, _, _, _ in diag_sets + prev_sets:
                scores()
            mlp(2)
            for _, suffixes, _, _ in diag_sets + prev_sets:
                suffixes()
                mlp(1)
            mlp(len(mlp_steps))
            diag_out = [values() for _, _, values, _ in diag_sets]
            for d_out, (_, _, values, _), group in zip(diag_out, prev_sets, groups):
                for (d_acc, _), (p_acc, carry), hp in zip(d_out, values(), group):
                    acc_ref[hp] = d_acc + p_acc
                    carry_ref[hp] = carry
        else:
            for step in mlp_steps:
                step()
        if with_mlp:
            mlp_finish()
        if with_attention:
            def more_to_do(state):
                t, top = state
                return jnp.logical_and(t < i - 1, top > WEIGHT_UNDERFLOW_LOG)

            def farther(state):
                t, _ = state
                k0 = pl.multiple_of((i - 2 - t) * blk, blk)
                sc, su, va, _ = attention_phases(qs, [(k0, None, None)],
                                                 lambda: [carry_ref[hp] for hp in pairs])
                sc()
                su()
                for hp, (pv, carry) in enumerate(va()):
                    acc_ref[hp] += pv
                    carry_ref[hp] = carry
                return t + 1, largest_carry()

            lax.while_loop(more_to_do, farther, (jnp.int32(0), largest_carry()))
            for hp in pairs:
                acc = acc_ref[hp]
                d_ref[slot, :, cols[hp]] = jnp.where(
                    first_head, acc[:blk], acc[blk:]).astype(d_ref.dtype)

    @pl.when(s == 0)
    def _():
        run_step(True, False)

    @pl.when(jnp.logical_and(s > 0, s < n_blocks))
    def _():
        run_step(True, True)

    @pl.when(s == n_blocks)
    def _():
        run_step(False, True)


def _attn_mlp_final(q, k, v, h, c, w_out, g, w1_layers, w2_layers, layer, final_g):
    B, T, W = q.shape
    N, D = h.shape
    blk = ATT_BLOCK
    assert T % blk == 0 and N == B * T and W % V7X_LANES == 0, (B, T, W, N)
    per_seq = T // blk
    n_blocks = N // blk
    neg_upper = -jnp.triu(jnp.ones((blk, blk), BF16), 1).T
    n_pairs = W // V7X_LANES

    def attn_block(s):
        return jnp.minimum(s, n_blocks - 1)

    def mlp_block(s):
        return jnp.maximum(s - 1, 0)

    q_spec = pl.BlockSpec((None, blk, W),
                          lambda s: (attn_block(s) // per_seq, attn_block(s) % per_seq, 0))
    kv_spec = pl.BlockSpec((None, T, W), lambda s: (attn_block(s) // per_seq, 0, 0))
    row_spec = pl.BlockSpec((blk, D), lambda s: (mlp_block(s), 0))
    half_spec = pl.BlockSpec((blk, W), lambda s: (mlp_block(s), 0))
    return pl.pallas_call(
        _attn_mlp_kernel,
        out_shape=jax.ShapeDtypeStruct((N, D), F32),
        grid=(n_blocks + 1,),
        in_specs=[q_spec, kv_spec, kv_spec, _resident(neg_upper.shape), row_spec, half_spec,
                  _resident(w_out.shape), _resident(g.shape),
                  _resident_layer(w1_layers.shape, layer),
                  _resident_layer(w2_layers.shape, layer), _resident(final_g.shape)],
        out_specs=row_spec,
        scratch_shapes=[pltpu.VMEM((n_pairs, 2 * blk, V7X_LANES), F32),
                        pltpu.VMEM((n_pairs, 2 * blk, V7X_LANES), F32),
                        pltpu.VMEM((2, blk, W), BF16)],
        compiler_params=pltpu.CompilerParams(
            dimension_semantics=("arbitrary",), vmem_limit_bytes=VMEM_LIMIT_BYTES),
        name="attention_out_mlp1_final",
    )(q, k, v, neg_upper, h, c, w_out, g, w1_layers, w2_layers, final_g)


def kernel(x, mix_norm_g, mlp_norm_g, ab_w_in, pool_w, pool_scale, conv_w, conv_b, ab_w_out,
           cd_w_in, sgu_norm_g, sgu_norm_b, sgu_w, sgu_b, cd_w_out, mlp_w1, mlp_w2, final_norm_g):
    B, T, D = x.shape
    N = B * T
    row = lambda vec: vec.reshape(1, -1)

    h = _l0_mixer(x, row(mix_norm_g[0]), ab_w_in[0], pool_w[0], pool_scale[0],
                  conv_w[0], row(conv_b[0]), ab_w_out[0])
    w1_layers, w2_layers = mlp_w1.astype(BF16), mlp_w2.astype(BF16)
    h = _mlp(h.reshape(N, D), row(mlp_norm_g[0]), w1_layers, w2_layers, 0)

    b_s = jnp.broadcast_to(sgu_b[0][:, :, None], sgu_b[0].shape + (GROUP_WIDTH,))
    c, q, k, v = _l1_front(h, row(mix_norm_g[1]), cd_w_in[0], row(sgu_norm_g[0]),
                           row(sgu_norm_b[0]), sgu_w[0], b_s)
    shape3 = (B, T, MIXER_WIDTH)
    out = _attn_mlp_final(q.reshape(shape3), k.reshape(shape3), v.reshape(shape3), h, c,
                          cd_w_out[0].astype(BF16), row(mlp_norm_g[1]), w1_layers, w2_layers, 1,
                          row(final_norm_g))
    return out.reshape(B, T, D)
```

```python
import functools
import math

import jax
import jax.numpy as jnp
from jax import lax
from jax.experimental import pallas as pl
from jax.experimental.pallas import tpu as pltpu

F32 = jnp.float32
BF16 = jnp.bfloat16

EPS = 1e-6
POOL_WINDOWS = (2, 4, 8, 16)
GROUP_WIDTH = 128
MIXER_WIDTH = 512
SGU_CHUNK = 128
SB_DH = 64

V7X_LANES = 128
V7X_SUBLANES = 8
V7X_VMEM_BYTES = 64 * 1024 * 1024
VMEM_LIMIT_BYTES = V7X_VMEM_BYTES - 8 * 1024 * 1024

ROW_TILE = 1024
L0_ROW_TILE = 512
POOL_HALO = 16
CONV_HALO = V7X_SUBLANES
ATT_BLOCK = 256
MLP_FF_CHUNK = 1024

LOG2E = math.log2(math.e)
WEIGHT_UNDERFLOW_LOG = -104.0
CHUNK_OFF = -1e30


def _resident(shape):
    zeros = (0,) * len(shape)
    return pl.BlockSpec(shape, lambda *_: zeros, pipeline_mode=pl.Buffered(1))


def _resident_layer(stacked_shape, layer):
    zeros = (0,) * (len(stacked_shape) - 1)
    return pl.BlockSpec((None,) + tuple(stacked_shape[1:]), lambda *_: (layer,) + zeros,
                        pipeline_mode=pl.Buffered(1))


def _rmsnorm(x, g):
    ms = jnp.mean(x * x, axis=-1, keepdims=True)
    return x * lax.rsqrt(ms + EPS) * g


def _trailing_window_sum(e, window):
    s = e
    span = 1
    while span < window:
        s = s + pltpu.roll(s, span, axis=0)
        span *= 2
    return s


def _layer0_kernel(x_ref, g_ref, win_ref, poolw_ref, pscale_ref, convw_ref, convb_ref, wout_ref,
                   mg_ref, w1_ref, w2_ref, o_ref, a_carry, u_carry, wmix_ref, win_bf16, ring_ref,
                   *, tiles_per_seq):
    s = pl.program_id(0)
    n_tiles = pl.num_programs(0) - 1
    tm = x_ref.shape[0]
    w = MIXER_WIDTH
    j = lax.rem(s, tiles_per_seq)
    slot = lax.rem(s, 2)

    @pl.when(s == 0)
    def _():
        win_bf16[...] = win_ref[...].astype(BF16)
        for grp in range(len(POOL_WINDOWS)):
            rows = slice(grp * GROUP_WIDTH, (grp + 1) * GROUP_WIDTH)
            scaled = poolw_ref[grp] * pscale_ref[grp:grp + 1, :]
            wmix_ref[rows, :] = jnp.dot(scaled, wout_ref[rows, :], preferred_element_type=F32,
                                        precision=lax.Precision.HIGHEST).astype(BF16)
        wmix_ref[w:, :] = wout_ref[w:, :].astype(BF16)

    @pl.when(j == 0)
    def _():
        a_carry[...] = jnp.zeros_like(a_carry)
        u_carry[...] = jnp.zeros_like(u_carry)

    def mixer_phases():
        st = {}

        def project():
            st["x"] = x_ref[...]
            xn = _rmsnorm(st["x"], g_ref[...]).astype(BF16)
            st["p"] = jnp.dot(xn, win_bf16[...], preferred_element_type=F32)

        def mix_and_store():
            p = st["p"]
            a, xb, gate_b, gate_c = p[:, :w], p[:, w:2 * w], p[:, 2 * w:3 * w], p[:, 3 * w:]
            a_ext = jnp.concatenate([a_carry[...], a], axis=0)
            a_carry[...] = a[tm - POOL_HALO:, :]
            pos = j * tm + lax.broadcasted_iota(jnp.int32, (tm, 1), 0)
            outs = []
            for grp, window in enumerate(POOL_WINDOWS):
                cols = slice(grp * GROUP_WIDTH, (grp + 1) * GROUP_WIDTH)
                wsum = _trailing_window_sum(a_ext[:, cols], window)[POOL_HALO:, :]
                count = jnp.minimum(pos + 1, window).astype(F32)
                outs.append(wsum / count - a[:, cols])
            u = gate_c * xb
            u_ext = jnp.concatenate([u_carry[...], u], axis=0)
            u_carry[...] = u[tm - CONV_HALO:, :]
            u1 = pltpu.roll(u_ext, 1, axis=0)[CONV_HALO:, :]
            u2 = pltpu.roll(u_ext, 2, axis=0)[CONV_HALO:, :]
            y = (u2 * convw_ref[0:1, :] + u1 * convw_ref[1:2, :] + u * convw_ref[2:3, :]
                 + convb_ref[...])
            outs.append(gate_b * y)
            mix_in = jnp.concatenate(outs, axis=1).astype(BF16)
            ring_ref[slot] = st["x"] + jnp.dot(mix_in, wmix_ref[...], preferred_element_type=F32)

        return project, mix_and_store

    def mlp_phases():
        st = {}
        d_ff = w1_ref.shape[1]
        chunk = min(MLP_FF_CHUNK, d_ff)

        def norm():
            h = ring_ref[1 - slot]
            st["hn"] = _rmsnorm(h, mg_ref[...]).astype(BF16)
            st["acc"] = h

        def up(c):
            def run():
                if c == 0:
                    norm()
                a = jnp.dot(st["hn"], w1_ref[:, c * chunk:(c + 1) * chunk],
                            preferred_element_type=F32)
                r = jnp.maximum(a, 0.0)
                st["r"] = (r * r).astype(BF16)
            return run

        def down(c):
            def run():
                st["acc"] = st["acc"] + jnp.dot(st["r"], w2_ref[c * chunk:(c + 1) * chunk, :],
                                                preferred_element_type=F32)
            return run

        def finish():
            o_ref[...] = st["acc"]

        steps = []
        for c in range(d_ff // chunk):
            steps += [up(c), down(c)]
        return steps, finish

    def run_step(with_mixer, with_mlp):
        mlp_steps, mlp_finish = mlp_phases() if with_mlp else ([], None)
        if with_mixer:
            project, mix_and_store = mixer_phases()
            for step in mlp_steps[:2]:
                step()
            project()
            for step in mlp_steps[2:6]:
                step()
            mix_and_store()
            for step in mlp_steps[6:]:
                step()
        else:
            for step in mlp_steps:
                step()
        if with_mlp:
            mlp_finish()

    @pl.when(s == 0)
    def _():
        run_step(True, False)

    @pl.when(jnp.logical_and(s > 0, s < n_tiles))
    def _():
        run_step(True, True)

    @pl.when(s == n_tiles)
    def _():
        run_step(False, True)


def _layer0(x, g, w_in, pool_w, pool_scale, conv_w, conv_b, w_out, mlp_g, w1_layers, w2_layers,
            layer):
    B, T, D = x.shape
    N = B * T
    tm = min(L0_ROW_TILE, T)
    assert T % tm == 0 and tm >= POOL_HALO, (T, tm)
    n_tiles = N // tm
    x_spec = pl.BlockSpec((tm, D), lambda s: (jnp.minimum(s, n_tiles - 1), 0))
    o_spec = pl.BlockSpec((tm, D), lambda s: (jnp.maximum(s - 1, 0), 0))
    return pl.pallas_call(
        functools.partial(_layer0_kernel, tiles_per_seq=T // tm),
        out_shape=jax.ShapeDtypeStruct((N, D), F32),
        grid=(n_tiles + 1,),
        in_specs=[x_spec, _resident(g.shape), _resident(w_in.shape), _resident(pool_w.shape),
                  _resident(pool_scale.shape), _resident(conv_w.shape), _resident(conv_b.shape),
                  _resident(w_out.shape), _resident(mlp_g.shape),
                  _resident_layer(w1_layers.shape, layer), _resident_layer(w2_layers.shape, layer)],
        out_specs=o_spec,
        scratch_shapes=[pltpu.VMEM((POOL_HALO, MIXER_WIDTH), F32),
                        pltpu.VMEM((CONV_HALO, MIXER_WIDTH), F32),
                        pltpu.VMEM(w_out.shape, BF16),
                        pltpu.VMEM(w_in.shape, BF16),
                        pltpu.VMEM((2, tm, D), F32)],
        compiler_params=pltpu.CompilerParams(
            dimension_semantics=("arbitrary",), vmem_limit_bytes=VMEM_LIMIT_BYTES),
        name="layer0_mixer_mlp",
    )(x.reshape(N, D), g, w_in, pool_w, pool_scale, conv_w, conv_b, w_out, mlp_g,
      w1_layers, w2_layers)


def _gelu(x):
    return 0.5 * x * (1.0 + lax.erf(x * (1.0 / math.sqrt(2.0))))


def _l1_front_kernel(h_ref, g_ref, win_ref, lng_ref, lnb_ref, ws_ref, bs_ref,
                     c_ref, q_ref, k_ref, v_ref, win_bf16):
    tm = h_ref.shape[0]
    w = MIXER_WIDTH

    @pl.when(pl.program_id(0) == 0)
    def _():
        win_bf16[...] = win_ref[...].astype(BF16)

    xn = _rmsnorm(h_ref[...], g_ref[...]).astype(BF16)
    p = jnp.dot(xn, win_bf16[...], preferred_element_type=F32)
    u = _gelu(p[:, :w])
    v = _gelu(p[:, w:2 * w])
    q_ref[...] = (p[:, 2 * w:3 * w] * (1.0 / math.sqrt(SB_DH))).astype(BF16)
    k_ref[...] = p[:, 3 * w:4 * w].astype(BF16)
    v_ref[...] = p[:, 4 * w:].astype(BF16)

    mu = jnp.mean(v, axis=-1, keepdims=True)
    vc = v - mu
    var = jnp.mean(vc * vc, axis=-1, keepdims=True)
    vn = (vc * lax.rsqrt(var + EPS) * lng_ref[...] + lnb_ref[...]).astype(BF16)

    L = SGU_CHUNK
    tpos = lax.broadcasted_iota(jnp.int32, (L, L), 0)
    spos = lax.broadcasted_iota(jnp.int32, (L, L), 1)
    causal = spos <= tpos
    for grp in range(w // GROUP_WIDTH):
        cols = slice(grp * GROUP_WIDTH, (grp + 1) * GROUP_WIDTH)
        w_s = jnp.where(causal, ws_ref[grp], 0.0).astype(BF16)
        bias = bs_ref[grp]
        for n in range(tm // L):
            rows = slice(n * L, (n + 1) * L)
            s = jnp.dot(w_s, vn[rows, cols], preferred_element_type=F32) + bias
            c_ref[rows, cols] = (u[rows, cols] * s).astype(BF16)


def _l1_front(h, g, w_in, ln_g, ln_b, w_s, b_s):
    N, D = h.shape
    tm = min(ROW_TILE, N)
    assert N % tm == 0 and tm % SGU_CHUNK == 0, (N, tm)
    row_spec = pl.BlockSpec((tm, D), lambda i: (i, 0))
    half_spec = pl.BlockSpec((tm, MIXER_WIDTH), lambda i: (i, 0))
    half_shape = jax.ShapeDtypeStruct((N, MIXER_WIDTH), BF16)
    return pl.pallas_call(
        _l1_front_kernel,
        out_shape=(half_shape,) * 4,
        grid=(N // tm,),
        in_specs=[row_spec, _resident(g.shape), _resident(w_in.shape), _resident(ln_g.shape),
                  _resident(ln_b.shape), _resident(w_s.shape), _resident(b_s.shape)],
        out_specs=(half_spec,) * 4,
        scratch_shapes=[pltpu.VMEM(w_in.shape, BF16)],
        compiler_params=pltpu.CompilerParams(
            dimension_semantics=("arbitrary",), vmem_limit_bytes=VMEM_LIMIT_BYTES),
        name="l1_front",
    )(h, g, w_in, ln_g, ln_b, w_s, b_s)


def _attn_mlp_kernel(q_ref, k_ref, v_ref, neg_upper_ref, h_ref, c_ref, wout_ref, g_ref, w1_ref,
                     w2_ref, fg_ref, o_ref, acc_ref, carry_ref, d_ref):
    s = pl.program_id(0)
    n_blocks = pl.num_programs(0) - 1
    blk = ATT_BLOCK
    T, W = k_ref.shape
    n_pairs = W // V7X_LANES
    pairs = range(n_pairs)
    cols = [slice(hp * V7X_LANES, (hp + 1) * V7X_LANES) for hp in pairs]
    lane = lax.broadcasted_iota(jnp.int32, (blk, V7X_LANES), 1)
    first_head = lane < SB_DH
    row_q = lax.broadcasted_iota(jnp.int32, (2 * blk, 1), 0) & (blk - 1)
    col_k = lax.broadcasted_iota(jnp.int32, (1, blk), 1)
    causal = col_k < row_q

    def attention_phases(qs, chunks, carry_in, hps=None):
        hps = list(pairs) if hps is None else hps
        chains = [(k0, mask, bias, hp) for k0, mask, bias in chunks for hp in hps]
        st = {}

        def scores():
            z = [lax.dot_general(qs[hp], k_ref[pl.ds(k0, blk), cols[hp]],
                                 (((1,), (1,)), ((), ())), preferred_element_type=F32)
                 for k0, _, _, hp in chains]
            st["log_beta"], st["keep_cost"], st["row_sum"] = [], [], []
            for (_, mask, _, _), zc in zip(chains, z):
                softplus = jnp.maximum(zc, 0.0) + jnp.log(1.0 + jnp.exp2(jnp.abs(zc) * -LOG2E))
                st["log_beta"].append(zc - softplus)
                if mask is not None:
                    softplus = jnp.where(mask, softplus, 0.0)
                st["keep_cost"].append(softplus.astype(BF16))
                st["row_sum"].append(jnp.broadcast_to(
                    jnp.sum(softplus, axis=1, keepdims=True), (2 * blk, V7X_LANES)))

        def suffixes():
            neg_upper = neg_upper_ref[...]
            suffix = [jnp.dot(kc, neg_upper, preferred_element_type=F32)
                      for kc in st["keep_cost"]]
            carry = list(carry_in())
            st["wgt"] = []
            for c, (_, mask, bias, hp) in enumerate(chains):
                logw = st["log_beta"][c] + suffix[c]
                chunk_carry = carry[hp]
                if bias is not None:
                    chunk_carry = bias if chunk_carry is None else chunk_carry + bias
                if chunk_carry is not None:
                    logw = logw + jnp.concatenate([chunk_carry] * (blk // V7X_LANES), axis=1)
                w = jnp.exp2(logw * LOG2E)
                if mask is not None:
                    w = jnp.where(mask, w, 0.0)
                st["wgt"].append(w.astype(BF16))
                row_sum = st["row_sum"][c]
                carry[hp] = -row_sum if chunk_carry is None else chunk_carry - row_sum
            st["carry"] = carry

        def values():
            out = []
            for hp in hps:
                acc = None
                for c, (k0, _, _, chain_hp) in enumerate(chains):
                    if chain_hp == hp:
                        pv = jnp.dot(st["wgt"][c], v_ref[pl.ds(k0, blk), cols[hp]],
                                     preferred_element_type=F32)
                        acc = pv if acc is None else acc + pv
                out.append((acc, st["carry"][hp]))
            return out

        return scores, suffixes, values, st

    def mlp_phases(slot):
        st = {}
        d_ff = w1_ref.shape[1]
        chunk = min(MLP_FF_CHUNK, d_ff)

        def out_proj():
            w = MIXER_WIDTH
            h = h_ref[...] + (jnp.dot(c_ref[...], wout_ref[:w, :], preferred_element_type=F32)
                              + jnp.dot(d_ref[slot], wout_ref[w:, :], preferred_element_type=F32))
            st["hn"] = _rmsnorm(h, g_ref[...]).astype(BF16)
            st["acc"] = h

        def up(c):
            def run():
                a = jnp.dot(st["hn"], w1_ref[:, c * chunk:(c + 1) * chunk],
                            preferred_element_type=F32)
                r = jnp.maximum(a, 0.0)
                st["r"] = (r * r).astype(BF16)
            return run

        def down(c):
            def run():
                st["acc"] = st["acc"] + jnp.dot(st["r"], w2_ref[c * chunk:(c + 1) * chunk, :],
                                                preferred_element_type=F32)
            return run

        def finish():
            o_ref[...] = _rmsnorm(st["acc"], fg_ref[...])

        steps = [out_proj]
        for c in range(d_ff // chunk):
            steps += [up(c), down(c)]
        return steps, finish

    def largest_carry():
        top = carry_ref[0]
        for hp in range(1, n_pairs):
            top = jnp.maximum(top, carry_ref[hp])
        return jnp.max(top)

    def run_step(with_attention, with_mlp):
        slot = lax.rem(s, 2)
        if with_mlp:
            mlp_steps, mlp_finish = mlp_phases(1 - slot)
        else:
            mlp_steps, mlp_finish = [], None
        if with_attention:
            i = lax.rem(s, T // blk)
            q0 = pl.multiple_of(i * blk, blk)
            qs = []
            for hp in pairs:
                q2 = q_ref[:, cols[hp]]
                zero = jnp.zeros_like(q2)
                qs.append(jnp.concatenate(
                    [jnp.where(first_head, q2, zero), jnp.where(first_head, zero, q2)], axis=0))
            prev0 = pl.multiple_of(jnp.maximum(q0 - blk, 0), blk)
            prev_bias = jnp.where(i == 0, CHUNK_OFF, 0.0).astype(F32)
            prev_bias = jnp.broadcast_to(prev_bias, (2 * blk, V7X_LANES))
            groups = [list(pairs)[:n_pairs // 2], list(pairs)[n_pairs // 2:]]
            diag_sets = [attention_phases(qs, [(q0, causal, None)], lambda: [None] * n_pairs, g)
                         for g in groups]

            def carry_after_diag(state, group):
                out = [None] * n_pairs
                for idx, hp in enumerate(group):
                    out[hp] = -state["row_sum"][idx]
                return out

            prev_sets = [attention_phases(qs, [(prev0, None, prev_bias)],
                                          lambda st=ds[3], g=g: carry_after_diag(st, g), g)
                         for ds, g in zip(diag_sets, groups)]
            mlp_steps = list(mlp_steps)

            def mlp(count):
                for _ in range(min(count, len(mlp_steps))):
                    mlp_steps.pop(0)()

            for scores, _, _, _ in diag_sets + prev_sets:
                scores()
            mlp(2)
            for _, suffixes, _, _ in diag_sets + prev_sets:
                suffixes()
                mlp(1)
            mlp(len(mlp_steps))
            diag_out = [values() for _, _, values, _ in diag_sets]
            for d_out, (_, _, values, _), group in zip(diag_out, prev_sets, groups):
                for (d_acc, _), (p_acc, carry), hp in zip(d_out, values(), group):
                    acc_ref[hp] = d_acc + p_acc
                    carry_ref[hp] = carry
        else:
            for step in mlp_steps:
                step()
        if with_mlp:
            mlp_finish()
        if with_attention:
            def more_to_do(state):
                t, top = state
                return jnp.logical_and(t < i - 1, top > WEIGHT_UNDERFLOW_LOG)

            def farther(state):
                t, _ = state
                k0 = pl.multiple_of((i - 2 - t) * blk, blk)
                sc, su, va, _ = attention_phases(qs, [(k0, None, None)],
                                                 lambda: [carry_ref[hp] for hp in pairs])
                sc()
                su()
                for hp, (pv, carry) in enumerate(va()):
                    acc_ref[hp] += pv
                    carry_ref[hp] = carry
                return t + 1, largest_carry()

            lax.while_loop(more_to_do, farther, (jnp.int32(0), largest_carry()))
            for hp in pairs:
                acc = acc_ref[hp]
                d_ref[slot, :, cols[hp]] = jnp.where(
                    first_head, acc[:blk], acc[blk:]).astype(d_ref.dtype)

    @pl.when(s == 0)
    def _():
        run_step(True, False)

    @pl.when(jnp.logical_and(s > 0, s < n_blocks))
    def _():
        run_step(True, True)

    @pl.when(s == n_blocks)
    def _():
        run_step(False, True)


def _attn_mlp_final(q, k, v, h, c, w_out, g, w1_layers, w2_layers, layer, final_g):
    B, T, W = q.shape
    N, D = h.shape
    blk = ATT_BLOCK
    assert T % blk == 0 and N == B * T and W % V7X_LANES == 0, (B, T, W, N)
    per_seq = T // blk
    n_blocks = N // blk
    neg_upper = -jnp.triu(jnp.ones((blk, blk), BF16), 1).T
    n_pairs = W // V7X_LANES

    def attn_block(s):
        return jnp.minimum(s, n_blocks - 1)

    def mlp_block(s):
        return jnp.maximum(s - 1, 0)

    q_spec = pl.BlockSpec((None, blk, W),
                          lambda s: (attn_block(s) // per_seq, attn_block(s) % per_seq, 0))
    kv_spec = pl.BlockSpec((None, T, W), lambda s: (attn_block(s) // per_seq, 0, 0))
    row_spec = pl.BlockSpec((blk, D), lambda s: (mlp_block(s), 0))
    half_spec = pl.BlockSpec((blk, W), lambda s: (mlp_block(s), 0))
    return pl.pallas_call(
        _attn_mlp_kernel,
        out_shape=jax.ShapeDtypeStruct((N, D), F32),
        grid=(n_blocks + 1,),
        in_specs=[q_spec, kv_spec, kv_spec, _resident(neg_upper.shape), row_spec, half_spec,
                  _resident(w_out.shape), _resident(g.shape),
                  _resident_layer(w1_layers.shape, layer),
                  _resident_layer(w2_layers.shape, layer), _resident(final_g.shape)],
        out_specs=row_spec,
        scratch_shapes=[pltpu.VMEM((n_pairs, 2 * blk, V7X_LANES), F32),
                        pltpu.VMEM((n_pairs, 2 * blk, V7X_LANES), F32),
                        pltpu.VMEM((2, blk, W), BF16)],
        compiler_params=pltpu.CompilerParams(
            dimension_semantics=("arbitrary",), vmem_limit_bytes=VMEM_LIMIT_BYTES),
        name="attention_out_mlp1_final",
    )(q, k, v, neg_upper, h, c, w_out, g, w1_layers, w2_layers, final_g)


def kernel(x, mix_norm_g, mlp_norm_g, ab_w_in, pool_w, pool_scale, conv_w, conv_b, ab_w_out,
           cd_w_in, sgu_norm_g, sgu_norm_b, sgu_w, sgu_b, cd_w_out, mlp_w1, mlp_w2, final_norm_g):
    B, T, D = x.shape
    N = B * T
    row = lambda vec: vec.reshape(1, -1)

    w1_layers, w2_layers = mlp_w1.astype(BF16), mlp_w2.astype(BF16)
    h = _layer0(x, row(mix_norm_g[0]), ab_w_in[0], pool_w[0], pool_scale[0], conv_w[0],
                row(conv_b[0]), ab_w_out[0], row(mlp_norm_g[0]), w1_layers, w2_layers, 0)

    b_s = jnp.broadcast_to(sgu_b[0][:, :, None], sgu_b[0].shape + (GROUP_WIDTH,))
    c, q, k, v = _l1_front(h, row(mix_norm_g[1]), cd_w_in[0], row(sgu_norm_g[0]),
                           row(sgu_norm_b[0]), sgu_w[0], b_s)
    shape3 = (B, T, MIXER_WIDTH)
    out = _attn_mlp_final(q.reshape(shape3), k.reshape(shape3), v.reshape(shape3), h, c,
                          cd_w_out[0].astype(BF16), row(mlp_norm_g[1]), w1_layers, w2_layers, 1,
                          row(final_norm_g))
    return out.reshape(B, T, D)
```

```python
import functools
import math

import jax
import jax.numpy as jnp
from jax import lax
from jax.experimental import pallas as pl
from jax.experimental.pallas import tpu as pltpu

F32 = jnp.float32
BF16 = jnp.bfloat16

EPS = 1e-6
POOL_WINDOWS = (2, 4, 8, 16)
GROUP_WIDTH = 128
MIXER_WIDTH = 512
SGU_CHUNK = 128
SB_DH = 64

V7X_LANES = 128
V7X_SUBLANES = 8
V7X_VMEM_BYTES = 64 * 1024 * 1024
VMEM_LIMIT_BYTES = V7X_VMEM_BYTES - 8 * 1024 * 1024

ROW_TILE = 1024
L0_ROW_TILE = 512
POOL_HALO = 16
CONV_HALO = V7X_SUBLANES
ATT_BLOCK = 256
MLP_FF_CHUNK = 1024

LOG2E = math.log2(math.e)
WEIGHT_UNDERFLOW_LOG = -104.0
CHUNK_OFF = -1e30


def _resident(shape):
    zeros = (0,) * len(shape)
    return pl.BlockSpec(shape, lambda *_: zeros, pipeline_mode=pl.Buffered(1))


def _resident_layer(stacked_shape, layer):
    zeros = (0,) * (len(stacked_shape) - 1)
    return pl.BlockSpec((None,) + tuple(stacked_shape[1:]), lambda *_: (layer,) + zeros,
                        pipeline_mode=pl.Buffered(1))


def _rmsnorm(x, g):
    ms = jnp.mean(x * x, axis=-1, keepdims=True)
    return x * lax.rsqrt(ms + EPS) * g


def _trailing_window_sum(e, window):
    s = e
    span = 1
    while span < window:
        s = s + pltpu.roll(s, span, axis=0)
        span *= 2
    return s


def _layer0_kernel(x_ref, g_ref, win_ref, poolw_ref, pscale_ref, convw_ref, convb_ref, wout_ref,
                   mg_ref, w1_ref, w2_ref, o_ref, a_carry, u_carry, wmix_ref, win_bf16, ring_ref,
                   *, tiles_per_seq):
    s = pl.program_id(0)
    n_tiles = pl.num_programs(0) - 1
    tm = x_ref.shape[0]
    w = MIXER_WIDTH
    j = lax.rem(s, tiles_per_seq)
    slot = lax.rem(s, 2)

    @pl.when(s == 0)
    def _():
        win_bf16[...] = win_ref[...].astype(BF16)
        for grp in range(len(POOL_WINDOWS)):
            rows = slice(grp * GROUP_WIDTH, (grp + 1) * GROUP_WIDTH)
            scaled = poolw_ref[grp] * pscale_ref[grp:grp + 1, :]
            wmix_ref[rows, :] = jnp.dot(scaled, wout_ref[rows, :], preferred_element_type=F32,
                                        precision=lax.Precision.HIGHEST).astype(BF16)
        wmix_ref[w:, :] = wout_ref[w:, :].astype(BF16)

    @pl.when(j == 0)
    def _():
        a_carry[...] = jnp.zeros_like(a_carry)
        u_carry[...] = jnp.zeros_like(u_carry)

    def mixer_phases():
        st = {}

        def project():
            st["x"] = x_ref[...]
            xn = _rmsnorm(st["x"], g_ref[...]).astype(BF16)
            st["p"] = jnp.dot(xn, win_bf16[...], preferred_element_type=F32)

        def mix_and_store():
            p = st["p"]
            a, xb, gate_b, gate_c = p[:, :w], p[:, w:2 * w], p[:, 2 * w:3 * w], p[:, 3 * w:]
            a_ext = jnp.concatenate([a_carry[...], a], axis=0)
            a_carry[...] = a[tm - POOL_HALO:, :]
            pos = j * tm + lax.broadcasted_iota(jnp.int32, (tm, 1), 0)
            outs = []
            for grp, window in enumerate(POOL_WINDOWS):
                cols = slice(grp * GROUP_WIDTH, (grp + 1) * GROUP_WIDTH)
                wsum = _trailing_window_sum(a_ext[:, cols], window)[POOL_HALO:, :]
                count = jnp.minimum(pos + 1, window).astype(F32)
                outs.append(wsum / count - a[:, cols])
            u = gate_c * xb
            u_ext = jnp.concatenate([u_carry[...], u], axis=0)
            u_carry[...] = u[tm - CONV_HALO:, :]
            u1 = pltpu.roll(u_ext, 1, axis=0)[CONV_HALO:, :]
            u2 = pltpu.roll(u_ext, 2, axis=0)[CONV_HALO:, :]
            y = (u2 * convw_ref[0:1, :] + u1 * convw_ref[1:2, :] + u * convw_ref[2:3, :]
                 + convb_ref[...])
            outs.append(gate_b * y)
            mix_in = jnp.concatenate(outs, axis=1).astype(BF16)
            ring_ref[slot] = st["x"] + jnp.dot(mix_in, wmix_ref[...], preferred_element_type=F32)

        return project, mix_and_store

    def mlp_phases():
        st = {}
        d_ff = w1_ref.shape[1]
        chunk = min(MLP_FF_CHUNK, d_ff)

        def norm():
            h = ring_ref[1 - slot]
            st["hn"] = _rmsnorm(h, mg_ref[...]).astype(BF16)
            st["acc"] = h

        def up(c):
            def run():
                if c == 0:
                    norm()
                a = jnp.dot(st["hn"], w1_ref[:, c * chunk:(c + 1) * chunk],
                            preferred_element_type=F32)
                r = jnp.maximum(a, 0.0)
                st["r"] = (r * r).astype(BF16)
            return run

        def down(c):
            def run():
                st["acc"] = st["acc"] + jnp.dot(st["r"], w2_ref[c * chunk:(c + 1) * chunk, :],
                                                preferred_element_type=F32)
            return run

        def finish():
            o_ref[...] = st["acc"]

        steps = []
        for c in range(d_ff // chunk):
            steps += [up(c), down(c)]
        return steps, finish

    def run_step(with_mixer, with_mlp):
        mlp_steps, mlp_finish = mlp_phases() if with_mlp else ([], None)
        if with_mixer:
            project, mix_and_store = mixer_phases()
            for step in mlp_steps[:2]:
                step()
            project()
            for step in mlp_steps[2:6]:
                step()
            mix_and_store()
            for step in mlp_steps[6:]:
                step()
        else:
            for step in mlp_steps:
                step()
        if with_mlp:
            mlp_finish()

    @pl.when(s == 0)
    def _():
        run_step(True, False)

    @pl.when(jnp.logical_and(s > 0, s < n_tiles))
    def _():
        run_step(True, True)

    @pl.when(s == n_tiles)
    def _():
        run_step(False, True)


def _layer0(x, g, w_in, pool_w, pool_scale, conv_w, conv_b, w_out, mlp_g, w1_layers, w2_layers,
            layer):
    B, T, D = x.shape
    N = B * T
    tm = min(L0_ROW_TILE, T)
    assert T % tm == 0 and tm >= POOL_HALO, (T, tm)
    n_tiles = N // tm
    x_spec = pl.BlockSpec((tm, D), lambda s: (jnp.minimum(s, n_tiles - 1), 0))
    o_spec = pl.BlockSpec((tm, D), lambda s: (jnp.maximum(s - 1, 0), 0))
    return pl.pallas_call(
        functools.partial(_layer0_kernel, tiles_per_seq=T // tm),
        out_shape=jax.ShapeDtypeStruct((N, D), F32),
        grid=(n_tiles + 1,),
        in_specs=[x_spec, _resident(g.shape), _resident(w_in.shape), _resident(pool_w.shape),
                  _resident(pool_scale.shape), _resident(conv_w.shape), _resident(conv_b.shape),
                  _resident(w_out.shape), _resident(mlp_g.shape),
                  _resident_layer(w1_layers.shape, layer), _resident_layer(w2_layers.shape, layer)],
        out_specs=o_spec,
        scratch_shapes=[pltpu.VMEM((POOL_HALO, MIXER_WIDTH), F32),
                        pltpu.VMEM((CONV_HALO, MIXER_WIDTH), F32),
                        pltpu.VMEM(w_out.shape, BF16),
                        pltpu.VMEM(w_in.shape, BF16),
                        pltpu.VMEM((2, tm, D), F32)],
        compiler_params=pltpu.CompilerParams(
            dimension_semantics=("arbitrary",), vmem_limit_bytes=VMEM_LIMIT_BYTES),
        name="layer0_mixer_mlp",
    )(x.reshape(N, D), g, w_in, pool_w, pool_scale, conv_w, conv_b, w_out, mlp_g,
      w1_layers, w2_layers)


def _gelu(x):
    return 0.5 * x * (1.0 + lax.erf(x * (1.0 / math.sqrt(2.0))))


def _l1_front_kernel(h_ref, g_ref, win_ref, lng_ref, lnb_ref, ws_ref, bs_ref,
                     c_ref, q_ref, k_ref, v_ref, win_bf16):
    tm = h_ref.shape[0]
    w = MIXER_WIDTH

    @pl.when(pl.program_id(0) == 0)
    def _():
        win_bf16[...] = win_ref[...].astype(BF16)

    xn = _rmsnorm(h_ref[...], g_ref[...]).astype(BF16)
    p = jnp.dot(xn, win_bf16[...], preferred_element_type=F32)
    u = _gelu(p[:, :w])
    v = _gelu(p[:, w:2 * w])
    q_ref[...] = (p[:, 2 * w:3 * w] * (1.0 / math.sqrt(SB_DH))).astype(BF16)
    k_ref[...] = p[:, 3 * w:4 * w].astype(BF16)
    v_ref[...] = p[:, 4 * w:].astype(BF16)

    mu = jnp.mean(v, axis=-1, keepdims=True)
    vc = v - mu
    var = jnp.mean(vc * vc, axis=-1, keepdims=True)
    vn = (vc * lax.rsqrt(var + EPS) * lng_ref[...] + lnb_ref[...]).astype(BF16)

    L = SGU_CHUNK
    tpos = lax.broadcasted_iota(jnp.int32, (L, L), 0)
    spos = lax.broadcasted_iota(jnp.int32, (L, L), 1)
    causal = spos <= tpos
    for grp in range(w // GROUP_WIDTH):
        cols = slice(grp * GROUP_WIDTH, (grp + 1) * GROUP_WIDTH)
        w_s = jnp.where(causal, ws_ref[grp], 0.0).astype(BF16)
        bias = bs_ref[grp]
        for n in range(tm // L):
            rows = slice(n * L, (n + 1) * L)
            s = jnp.dot(w_s, vn[rows, cols], preferred_element_type=F32) + bias
            c_ref[rows, cols] = (u[rows, cols] * s).astype(BF16)


def _l1_front(h, g, w_in, ln_g, ln_b, w_s, b_s):
    N, D = h.shape
    tm = min(ROW_TILE, N)
    assert N % tm == 0 and tm % SGU_CHUNK == 0, (N, tm)
    row_spec = pl.BlockSpec((tm, D), lambda i: (i, 0))
    half_spec = pl.BlockSpec((tm, MIXER_WIDTH), lambda i: (i, 0))
    half_shape = jax.ShapeDtypeStruct((N, MIXER_WIDTH), BF16)
    return pl.pallas_call(
        _l1_front_kernel,
        out_shape=(half_shape,) * 4,
        grid=(N // tm,),
        in_specs=[row_spec, _resident(g.shape), _resident(w_in.shape), _resident(ln_g.shape),
                  _resident(ln_b.shape), _resident(w_s.shape), _resident(b_s.shape)],
        out_specs=(half_spec,) * 4,
        scratch_shapes=[pltpu.VMEM(w_in.shape, BF16)],
        compiler_params=pltpu.CompilerParams(
            dimension_semantics=("arbitrary",), vmem_limit_bytes=VMEM_LIMIT_BYTES),
        name="l1_front",
    )(h, g, w_in, ln_g, ln_b, w_s, b_s)


def _attn_mlp_kernel(q_ref, k_ref, v_ref, neg_upper_ref, h_ref, c_ref, wout_ref, g_ref, w1_ref,
                     w2_ref, fg_ref, o_ref, acc_ref, carry_ref, d_ref):
    s = pl.program_id(0)
    n_blocks = pl.num_programs(0) - 1
    blk = ATT_BLOCK
    T, W = k_ref.shape
    n_pairs = W // V7X_LANES
    pairs = range(n_pairs)
    cols = [slice(hp * V7X_LANES, (hp + 1) * V7X_LANES) for hp in pairs]
    lane = lax.broadcasted_iota(jnp.int32, (blk, V7X_LANES), 1)
    first_head = lane < SB_DH
    row_q = lax.broadcasted_iota(jnp.int32, (2 * blk, 1), 0) & (blk - 1)
    col_k = lax.broadcasted_iota(jnp.int32, (1, blk), 1)
    causal = col_k < row_q

    def attention_phases(qs, chunks, carry_in, hps=None):
        hps = list(pairs) if hps is None else hps
        chains = [(k0, mask, bias, hp) for k0, mask, bias in chunks for hp in hps]
        st = {}

        def scores():
            z = [lax.dot_general(qs[hp], k_ref[pl.ds(k0, blk), cols[hp]],
                                 (((1,), (1,)), ((), ())), preferred_element_type=F32)
                 for k0, _, _, hp in chains]
            st["log_beta"], st["keep_cost"], st["row_sum"] = [], [], []
            for (_, mask, _, _), zc in zip(chains, z):
                softplus = jnp.maximum(zc, 0.0) + jnp.log(1.0 + jnp.exp2(jnp.abs(zc) * -LOG2E))
                st["log_beta"].append(zc - softplus)
                if mask is not None:
                    softplus = jnp.where(mask, softplus, 0.0)
                st["keep_cost"].append(softplus.astype(BF16))
                st["row_sum"].append(jnp.broadcast_to(
                    jnp.sum(softplus, axis=1, keepdims=True), (2 * blk, V7X_LANES)))

        def suffixes():
            neg_upper = neg_upper_ref[...]
            suffix = [jnp.dot(kc, neg_upper, preferred_element_type=F32)
                      for kc in st["keep_cost"]]
            carry = list(carry_in())
            st["wgt"] = []
            for c, (_, mask, bias, hp) in enumerate(chains):
                logw = st["log_beta"][c] + suffix[c]
                chunk_carry = carry[hp]
                if bias is not None:
                    chunk_carry = bias if chunk_carry is None else chunk_carry + bias
                if chunk_carry is not None:
                    logw = logw + jnp.concatenate([chunk_carry] * (blk // V7X_LANES), axis=1)
                w = jnp.exp2(logw * LOG2E)
                if mask is not None:
                    w = jnp.where(mask, w, 0.0)
                st["wgt"].append(w.astype(BF16))
                row_sum = st["row_sum"][c]
                carry[hp] = -row_sum if chunk_carry is None else chunk_carry - row_sum
            st["carry"] = carry

        def values():
            out = []
            for hp in hps:
                acc = None
                for c, (k0, _, _, chain_hp) in enumerate(chains):
                    if chain_hp == hp:
                        pv = jnp.dot(st["wgt"][c], v_ref[pl.ds(k0, blk), cols[hp]],
                                     preferred_element_type=F32)
                        acc = pv if acc is None else acc + pv
                out.append((acc, st["carry"][hp]))
            return out

        return scores, suffixes, values, st

    def mlp_phases(slot):
        st = {}
        d_ff = w1_ref.shape[1]
        chunk = min(MLP_FF_CHUNK, d_ff)

        def out_proj():
            w = MIXER_WIDTH
            h = h_ref[...] + (jnp.dot(c_ref[...], wout_ref[:w, :], preferred_element_type=F32)
                              + jnp.dot(d_ref[slot], wout_ref[w:, :], preferred_element_type=F32))
            st["hn"] = _rmsnorm(h, g_ref[...]).astype(BF16)
            st["acc"] = h

        def up(c):
            def run():
                a = jnp.dot(st["hn"], w1_ref[:, c * chunk:(c + 1) * chunk],
                            preferred_element_type=F32)
                r = jnp.maximum(a, 0.0)
                st["r"] = (r * r).astype(BF16)
            return run

        def down(c):
            def run():
                st["acc"] = st["acc"] + jnp.dot(st["r"], w2_ref[c * chunk:(c + 1) * chunk, :],
                                                preferred_element_type=F32)
            return run

        def finish():
            o_ref[...] = _rmsnorm(st["acc"], fg_ref[...])

        steps = [out_proj]
        for c in range(d_ff // chunk):
            steps += [up(c), down(c)]
        return steps, finish

    def largest_carry():
        top = carry_ref[0]
        for hp in range(1, n_pairs):
            top = jnp.maximum(top, carry_ref[hp])
        return jnp.max(top)

    def run_step(with_attention, with_mlp):
        slot = lax.rem(s, 2)
        if with_mlp:
            mlp_steps, mlp_finish = mlp_phases(1 - slot)
        else:
            mlp_steps, mlp_finish = [], None
        def store_block(hp, acc):
            d_ref[slot, :, cols[hp]] = jnp.where(
                first_head, acc[:blk], acc[blk:]).astype(d_ref.dtype)

        if with_attention:
            i = lax.rem(s, T // blk)
            q0 = pl.multiple_of(i * blk, blk)
            qs = []
            for hp in pairs:
                q2 = q_ref[:, cols[hp]]
                zero = jnp.zeros_like(q2)
                qs.append(jnp.concatenate(
                    [jnp.where(first_head, q2, zero), jnp.where(first_head, zero, q2)], axis=0))
            prev0 = pl.multiple_of(jnp.maximum(q0 - blk, 0), blk)
            prev_bias = jnp.where(i == 0, CHUNK_OFF, 0.0).astype(F32)
            prev_bias = jnp.broadcast_to(prev_bias, (2 * blk, V7X_LANES))
            groups = [list(pairs)[:n_pairs // 2], list(pairs)[n_pairs // 2:]]
            diag_sets = [attention_phases(qs, [(q0, causal, None)], lambda: [None] * n_pairs, g)
                         for g in groups]

            def carry_after_diag(state, group):
                out = [None] * n_pairs
                for idx, hp in enumerate(group):
                    out[hp] = -state["row_sum"][idx]
                return out

            prev_sets = [attention_phases(qs, [(prev0, None, prev_bias)],
                                          lambda st=ds[3], g=g: carry_after_diag(st, g), g)
                         for ds, g in zip(diag_sets, groups)]
            mlp_steps = list(mlp_steps)

            def mlp(count):
                for _ in range(min(count, len(mlp_steps))):
                    mlp_steps.pop(0)()

            for scores, _, _, _ in diag_sets + prev_sets:
                scores()
            mlp(2)
            for _, suffixes, _, _ in diag_sets + prev_sets:
                suffixes()
                mlp(1)
            mlp(len(mlp_steps))
            diag_out = [values() for _, _, values, _ in diag_sets]
            top = None
            for d_out, (_, _, values, _), group in zip(diag_out, prev_sets, groups):
                for (d_acc, _), (p_acc, carry), hp in zip(d_out, values(), group):
                    acc = d_acc + p_acc
                    acc_ref[hp] = acc
                    carry_ref[hp] = carry
                    store_block(hp, acc)
                    top = carry if top is None else jnp.maximum(top, carry)
            top = jnp.max(top)
        else:
            for step in mlp_steps:
                step()
        if with_mlp:
            mlp_finish()
        if with_attention:
            def more_to_do(state):
                t, top = state
                return jnp.logical_and(t < i - 1, top > WEIGHT_UNDERFLOW_LOG)

            def farther(state):
                t, _ = state
                k0 = pl.multiple_of((i - 2 - t) * blk, blk)
                sc, su, va, _ = attention_phases(qs, [(k0, None, None)],
                                                 lambda: [carry_ref[hp] for hp in pairs])
                sc()
                su()
                for hp, (pv, carry) in enumerate(va()):
                    acc_ref[hp] += pv
                    carry_ref[hp] = carry
                return t + 1, largest_carry()

            walked, _ = lax.while_loop(more_to_do, farther, (jnp.int32(0), top))

            @pl.when(walked > 0)
            def _():
                for hp in pairs:
                    store_block(hp, acc_ref[hp])

    @pl.when(s == 0)
    def _():
        run_step(True, False)

    @pl.when(jnp.logical_and(s > 0, s < n_blocks))
    def _():
        run_step(True, True)

    @pl.when(s == n_blocks)
    def _():
        run_step(False, True)


def _attn_mlp_final(q, k, v, h, c, w_out, g, w1_layers, w2_layers, layer, final_g):
    B, T, W = q.shape
    N, D = h.shape
    blk = ATT_BLOCK
    assert T % blk == 0 and N == B * T and W % V7X_LANES == 0, (B, T, W, N)
    per_seq = T // blk
    n_blocks = N // blk
    neg_upper = -jnp.triu(jnp.ones((blk, blk), BF16), 1).T
    n_pairs = W // V7X_LANES

    def attn_block(s):
        return jnp.minimum(s, n_blocks - 1)

    def mlp_block(s):
        return jnp.maximum(s - 1, 0)

    q_spec = pl.BlockSpec((None, blk, W),
                          lambda s: (attn_block(s) // per_seq, attn_block(s) % per_seq, 0))
    kv_spec = pl.BlockSpec((None, T, W), lambda s: (attn_block(s) // per_seq, 0, 0))
    row_spec = pl.BlockSpec((blk, D), lambda s: (mlp_block(s), 0))
    half_spec = pl.BlockSpec((blk, W), lambda s: (mlp_block(s), 0))
    return pl.pallas_call(
        _attn_mlp_kernel,
        out_shape=jax.ShapeDtypeStruct((N, D), F32),
        grid=(n_blocks + 1,),
        in_specs=[q_spec, kv_spec, kv_spec, _resident(neg_upper.shape), row_spec, half_spec,
                  _resident(w_out.shape), _resident(g.shape),
                  _resident_layer(w1_layers.shape, layer),
                  _resident_layer(w2_layers.shape, layer), _resident(final_g.shape)],
        out_specs=row_spec,
        scratch_shapes=[pltpu.VMEM((n_pairs, 2 * blk, V7X_LANES), F32),
                        pltpu.VMEM((n_pairs, 2 * blk, V7X_LANES), F32),
                        pltpu.VMEM((2, blk, W), BF16)],
        compiler_params=pltpu.CompilerParams(
            dimension_semantics=("arbitrary",), vmem_limit_bytes=VMEM_LIMIT_BYTES),
        name="attention_out_mlp1_final",
    )(q, k, v, neg_upper, h, c, w_out, g, w1_layers, w2_layers, final_g)


def kernel(x, mix_norm_g, mlp_norm_g, ab_w_in, pool_w, pool_scale, conv_w, conv_b, ab_w_out,
           cd_w_in, sgu_norm_g, sgu_norm_b, sgu_w, sgu_b, cd_w_out, mlp_w1, mlp_w2, final_norm_g):
    B, T, D = x.shape
    N = B * T
    row = lambda vec: vec.reshape(1, -1)

    w1_layers, w2_layers = mlp_w1.astype(BF16), mlp_w2.astype(BF16)
    h = _layer0(x, row(mix_norm_g[0]), ab_w_in[0], pool_w[0], pool_scale[0], conv_w[0],
                row(conv_b[0]), ab_w_out[0], row(mlp_norm_g[0]), w1_layers, w2_layers, 0)

    b_s = jnp.broadcast_to(sgu_b[0][:, :, None], sgu_b[0].shape + (GROUP_WIDTH,))
    c, q, k, v = _l1_front(h, row(mix_norm_g[1]), cd_w_in[0], row(sgu_norm_g[0]),
                           row(sgu_norm_b[0]), sgu_w[0], b_s)
    shape3 = (B, T, MIXER_WIDTH)
    out = _attn_mlp_final(q.reshape(shape3), k.reshape(shape3), v.reshape(shape3), h, c,
                          cd_w_out[0].astype(BF16), row(mlp_norm_g[1]), w1_layers, w2_layers, 1,
                          row(final_norm_g))
    return out.reshape(B, T, D)
```

```python
import functools
import math

import jax
import jax.numpy as jnp
from jax import lax
from jax.experimental import pallas as pl
from jax.experimental.pallas import tpu as pltpu

F32 = jnp.float32
BF16 = jnp.bfloat16

EPS = 1e-6
POOL_WINDOWS = (2, 4, 8, 16)
GROUP_WIDTH = 128
MIXER_WIDTH = 512
SGU_CHUNK = 128
SB_DH = 64

V7X_LANES = 128
V7X_SUBLANES = 8
V7X_VMEM_BYTES = 64 * 1024 * 1024
VMEM_LIMIT_BYTES = V7X_VMEM_BYTES - 8 * 1024 * 1024

ROW_TILE = 1024
L0_ROW_TILE = 512
POOL_HALO = 16
CONV_HALO = V7X_SUBLANES
ATT_BLOCK = 256
MLP_FF_CHUNK = 1024

LOG2E = math.log2(math.e)
WEIGHT_UNDERFLOW_LOG = -104.0
CHUNK_OFF = -1e30


def _resident(shape):
    zeros = (0,) * len(shape)
    return pl.BlockSpec(shape, lambda *_: zeros, pipeline_mode=pl.Buffered(1))


def _resident_layer(stacked_shape, layer):
    zeros = (0,) * (len(stacked_shape) - 1)
    return pl.BlockSpec((None,) + tuple(stacked_shape[1:]), lambda *_: (layer,) + zeros,
                        pipeline_mode=pl.Buffered(1))


def _rmsnorm(x, g):
    ms = jnp.mean(x * x, axis=-1, keepdims=True)
    return x * lax.rsqrt(ms + EPS) * g


def _trailing_window_sum(e, window):
    s = e
    span = 1
    while span < window:
        s = s + pltpu.roll(s, span, axis=0)
        span *= 2
    return s


def _layer0_kernel(x_ref, g_ref, win_ref, poolw_ref, pscale_ref, convw_ref, convb_ref, wout_ref,
                   mg_ref, w1_ref, w2_ref, o_ref, a_carry, u_carry, wmix_ref, win_bf16, ring_ref,
                   *, tiles_per_seq):
    s = pl.program_id(0)
    n_tiles = pl.num_programs(0) - 1
    tm = x_ref.shape[0]
    w = MIXER_WIDTH
    j = lax.rem(s, tiles_per_seq)
    slot = lax.rem(s, 2)

    @pl.when(s == 0)
    def _():
        win_bf16[...] = win_ref[...].astype(BF16)
        for grp in range(len(POOL_WINDOWS)):
            rows = slice(grp * GROUP_WIDTH, (grp + 1) * GROUP_WIDTH)
            scaled = poolw_ref[grp] * pscale_ref[grp:grp + 1, :]
            wmix_ref[rows, :] = jnp.dot(scaled, wout_ref[rows, :], preferred_element_type=F32,
                                        precision=lax.Precision.HIGHEST).astype(BF16)
        wmix_ref[w:, :] = wout_ref[w:, :].astype(BF16)

    @pl.when(j == 0)
    def _():
        a_carry[...] = jnp.zeros_like(a_carry)
        u_carry[...] = jnp.zeros_like(u_carry)

    def mixer_phases():
        st = {}

        def project():
            st["x"] = x_ref[...]
            xn = _rmsnorm(st["x"], g_ref[...]).astype(BF16)
            st["p"] = jnp.dot(xn, win_bf16[...], preferred_element_type=F32)

        def mix_and_store():
            p = st["p"]
            a, xb, gate_b, gate_c = p[:, :w], p[:, w:2 * w], p[:, 2 * w:3 * w], p[:, 3 * w:]
            a_ext = jnp.concatenate([a_carry[...], a], axis=0)
            a_carry[...] = a[tm - POOL_HALO:, :]
            pos = j * tm + lax.broadcasted_iota(jnp.int32, (tm, 1), 0)
            outs = []
            for grp, window in enumerate(POOL_WINDOWS):
                cols = slice(grp * GROUP_WIDTH, (grp + 1) * GROUP_WIDTH)
                wsum = _trailing_window_sum(a_ext[:, cols], window)[POOL_HALO:, :]
                count = jnp.minimum(pos + 1, window).astype(F32)
                outs.append(wsum / count - a[:, cols])
            u = gate_c * xb
            u_ext = jnp.concatenate([u_carry[...], u], axis=0)
            u_carry[...] = u[tm - CONV_HALO:, :]
            u1 = pltpu.roll(u_ext, 1, axis=0)[CONV_HALO:, :]
            u2 = pltpu.roll(u_ext, 2, axis=0)[CONV_HALO:, :]
            y = (u2 * convw_ref[0:1, :] + u1 * convw_ref[1:2, :] + u * convw_ref[2:3, :]
                 + convb_ref[...])
            outs.append(gate_b * y)
            mix_in = jnp.concatenate(outs, axis=1).astype(BF16)
            ring_ref[slot] = st["x"] + jnp.dot(mix_in, wmix_ref[...], preferred_element_type=F32)

        return project, mix_and_store

    def mlp_phases():
        st = {}
        d_ff = w1_ref.shape[1]
        chunk = min(MLP_FF_CHUNK, d_ff)

        def norm():
            h = ring_ref[1 - slot]
            st["hn"] = _rmsnorm(h, mg_ref[...]).astype(BF16)
            st["acc"] = h

        def up(c):
            def run():
                if c == 0:
                    norm()
                a = jnp.dot(st["hn"], w1_ref[:, c * chunk:(c + 1) * chunk],
                            preferred_element_type=F32)
                r = jnp.maximum(a, 0.0)
                st["r"] = (r * r).astype(BF16)
            return run

        def down(c):
            def run():
                st["acc"] = st["acc"] + jnp.dot(st["r"], w2_ref[c * chunk:(c + 1) * chunk, :],
                                                preferred_element_type=F32)
            return run

        def finish():
            o_ref[...] = st["acc"]

        steps = []
        for c in range(d_ff // chunk):
            steps += [up(c), down(c)]
        return steps, finish

    def run_step(with_mixer, with_mlp):
        mlp_steps, mlp_finish = mlp_phases() if with_mlp else ([], None)
        if with_mixer:
            project, mix_and_store = mixer_phases()
            for step in mlp_steps[:2]:
                step()
            project()
            for step in mlp_steps[2:6]:
                step()
            mix_and_store()
            for step in mlp_steps[6:]:
                step()
        else:
            for step in mlp_steps:
                step()
        if with_mlp:
            mlp_finish()

    @pl.when(s == 0)
    def _():
        run_step(True, False)

    @pl.when(jnp.logical_and(s > 0, s < n_tiles))
    def _():
        run_step(True, True)

    @pl.when(s == n_tiles)
    def _():
        run_step(False, True)


def _layer0(x, g, w_in, pool_w, pool_scale, conv_w, conv_b, w_out, mlp_g, w1_layers, w2_layers,
            layer):
    B, T, D = x.shape
    N = B * T
    tm = min(L0_ROW_TILE, T)
    assert T % tm == 0 and tm >= POOL_HALO, (T, tm)
    n_tiles = N // tm
    x_spec = pl.BlockSpec((tm, D), lambda s: (jnp.minimum(s, n_tiles - 1), 0))
    o_spec = pl.BlockSpec((tm, D), lambda s: (jnp.maximum(s - 1, 0), 0))
    return pl.pallas_call(
        functools.partial(_layer0_kernel, tiles_per_seq=T // tm),
        out_shape=jax.ShapeDtypeStruct((N, D), F32),
        grid=(n_tiles + 1,),
        in_specs=[x_spec, _resident(g.shape), _resident(w_in.shape), _resident(pool_w.shape),
                  _resident(pool_scale.shape), _resident(conv_w.shape), _resident(conv_b.shape),
                  _resident(w_out.shape), _resident(mlp_g.shape),
                  _resident_layer(w1_layers.shape, layer), _resident_layer(w2_layers.shape, layer)],
        out_specs=o_spec,
        scratch_shapes=[pltpu.VMEM((POOL_HALO, MIXER_WIDTH), F32),
                        pltpu.VMEM((CONV_HALO, MIXER_WIDTH), F32),
                        pltpu.VMEM(w_out.shape, BF16),
                        pltpu.VMEM(w_in.shape, BF16),
                        pltpu.VMEM((2, tm, D), F32)],
        compiler_params=pltpu.CompilerParams(
            dimension_semantics=("arbitrary",), vmem_limit_bytes=VMEM_LIMIT_BYTES),
        name="layer0_mixer_mlp",
    )(x.reshape(N, D), g, w_in, pool_w, pool_scale, conv_w, conv_b, w_out, mlp_g,
      w1_layers, w2_layers)


def _gelu(x):
    return 0.5 * x * (1.0 + lax.erf(x * (1.0 / math.sqrt(2.0))))


def _l1_front_kernel(h_ref, g_ref, win_ref, lng_ref, lnb_ref, ws_ref, bs_ref,
                     c_ref, q_ref, k_ref, v_ref, win_bf16):
    tm = h_ref.shape[0]
    w = MIXER_WIDTH

    @pl.when(pl.program_id(0) == 0)
    def _():
        win_bf16[...] = win_ref[...].astype(BF16)

    xn = _rmsnorm(h_ref[...], g_ref[...]).astype(BF16)
    p = jnp.dot(xn, win_bf16[...], preferred_element_type=F32)
    u = _gelu(p[:, :w])
    v = _gelu(p[:, w:2 * w])
    q_ref[...] = (p[:, 2 * w:3 * w] * (1.0 / math.sqrt(SB_DH))).astype(BF16)
    k_ref[...] = p[:, 3 * w:4 * w].astype(BF16)
    v_ref[...] = p[:, 4 * w:].astype(BF16)

    mu = jnp.mean(v, axis=-1, keepdims=True)
    vc = v - mu
    var = jnp.mean(vc * vc, axis=-1, keepdims=True)
    vn = (vc * lax.rsqrt(var + EPS) * lng_ref[...] + lnb_ref[...]).astype(BF16)

    L = SGU_CHUNK
    tpos = lax.broadcasted_iota(jnp.int32, (L, L), 0)
    spos = lax.broadcasted_iota(jnp.int32, (L, L), 1)
    causal = spos <= tpos
    for grp in range(w // GROUP_WIDTH):
        cols = slice(grp * GROUP_WIDTH, (grp + 1) * GROUP_WIDTH)
        w_s = jnp.where(causal, ws_ref[grp], 0.0).astype(BF16)
        bias = bs_ref[grp]
        for n in range(tm // L):
            rows = slice(n * L, (n + 1) * L)
            s = jnp.dot(w_s, vn[rows, cols], preferred_element_type=F32) + bias
            c_ref[rows, cols] = (u[rows, cols] * s).astype(BF16)


def _l1_front(h, g, w_in, ln_g, ln_b, w_s, b_s):
    N, D = h.shape
    tm = min(ROW_TILE, N)
    assert N % tm == 0 and tm % SGU_CHUNK == 0, (N, tm)
    row_spec = pl.BlockSpec((tm, D), lambda i: (i, 0))
    half_spec = pl.BlockSpec((tm, MIXER_WIDTH), lambda i: (i, 0))
    half_shape = jax.ShapeDtypeStruct((N, MIXER_WIDTH), BF16)
    return pl.pallas_call(
        _l1_front_kernel,
        out_shape=(half_shape,) * 4,
        grid=(N // tm,),
        in_specs=[row_spec, _resident(g.shape), _resident(w_in.shape), _resident(ln_g.shape),
                  _resident(ln_b.shape), _resident(w_s.shape), _resident(b_s.shape)],
        out_specs=(half_spec,) * 4,
        scratch_shapes=[pltpu.VMEM(w_in.shape, BF16)],
        compiler_params=pltpu.CompilerParams(
            dimension_semantics=("arbitrary",), vmem_limit_bytes=VMEM_LIMIT_BYTES),
        name="l1_front",
    )(h, g, w_in, ln_g, ln_b, w_s, b_s)


def _attn_mlp_kernel(q_ref, k_ref, v_ref, neg_upper_ref, h_ref, c_ref, wout_ref, g_ref, w1_ref,
                     w2_ref, fg_ref, o_ref, acc_ref, carry_ref, d_ref, wout_bf16):
    s = pl.program_id(0)
    n_blocks = pl.num_programs(0) - 1
    blk = ATT_BLOCK
    T, W = k_ref.shape
    n_pairs = W // V7X_LANES
    pairs = range(n_pairs)
    cols = [slice(hp * V7X_LANES, (hp + 1) * V7X_LANES) for hp in pairs]
    lane = lax.broadcasted_iota(jnp.int32, (blk, V7X_LANES), 1)
    first_head = lane < SB_DH
    row_q = lax.broadcasted_iota(jnp.int32, (2 * blk, 1), 0) & (blk - 1)
    col_k = lax.broadcasted_iota(jnp.int32, (1, blk), 1)
    causal = col_k < row_q

    def attention_phases(qs, chunks, carry_in, hps=None):
        hps = list(pairs) if hps is None else hps
        chains = [(k0, mask, bias, hp) for k0, mask, bias in chunks for hp in hps]
        st = {}

        def scores():
            z = [lax.dot_general(qs[hp], k_ref[pl.ds(k0, blk), cols[hp]],
                                 (((1,), (1,)), ((), ())), preferred_element_type=F32)
                 for k0, _, _, hp in chains]
            st["log_beta"], st["keep_cost"], st["row_sum"] = [], [], []
            for (_, mask, _, _), zc in zip(chains, z):
                softplus = jnp.maximum(zc, 0.0) + jnp.log(1.0 + jnp.exp2(jnp.abs(zc) * -LOG2E))
                st["log_beta"].append(zc - softplus)
                if mask is not None:
                    softplus = jnp.where(mask, softplus, 0.0)
                st["keep_cost"].append(softplus.astype(BF16))
                st["row_sum"].append(jnp.broadcast_to(
                    jnp.sum(softplus, axis=1, keepdims=True), (2 * blk, V7X_LANES)))

        def suffixes():
            neg_upper = neg_upper_ref[...]
            suffix = [jnp.dot(kc, neg_upper, preferred_element_type=F32)
                      for kc in st["keep_cost"]]
            carry = list(carry_in())
            st["wgt"] = []
            for c, (_, mask, bias, hp) in enumerate(chains):
                logw = st["log_beta"][c] + suffix[c]
                chunk_carry = carry[hp]
                if bias is not None:
                    chunk_carry = bias if chunk_carry is None else chunk_carry + bias
                if chunk_carry is not None:
                    logw = logw + jnp.concatenate([chunk_carry] * (blk // V7X_LANES), axis=1)
                w = jnp.exp2(logw * LOG2E)
                if mask is not None:
                    w = jnp.where(mask, w, 0.0)
                st["wgt"].append(w.astype(BF16))
                row_sum = st["row_sum"][c]
                carry[hp] = -row_sum if chunk_carry is None else chunk_carry - row_sum
            st["carry"] = carry

        def values():
            out = []
            for hp in hps:
                acc = None
                for c, (k0, _, _, chain_hp) in enumerate(chains):
                    if chain_hp == hp:
                        pv = jnp.dot(st["wgt"][c], v_ref[pl.ds(k0, blk), cols[hp]],
                                     preferred_element_type=F32)
                        acc = pv if acc is None else acc + pv
                out.append((acc, st["carry"][hp]))
            return out

        return scores, suffixes, values, st

    def mlp_phases(slot):
        st = {}
        d_ff = w1_ref.shape[1]
        chunk = min(MLP_FF_CHUNK, d_ff)

        def out_proj():
            w = MIXER_WIDTH
            h = h_ref[...] + (jnp.dot(c_ref[...], wout_bf16[:w, :], preferred_element_type=F32)
                              + jnp.dot(d_ref[slot], wout_bf16[w:, :], preferred_element_type=F32))
            st["hn"] = _rmsnorm(h, g_ref[...]).astype(BF16)
            st["acc"] = h

        def up(c):
            def run():
                a = jnp.dot(st["hn"], w1_ref[:, c * chunk:(c + 1) * chunk],
                            preferred_element_type=F32)
                r = jnp.maximum(a, 0.0)
                st["r"] = (r * r).astype(BF16)
            return run

        def down(c):
            def run():
                st["acc"] = st["acc"] + jnp.dot(st["r"], w2_ref[c * chunk:(c + 1) * chunk, :],
                                                preferred_element_type=F32)
            return run

        def finish():
            o_ref[...] = _rmsnorm(st["acc"], fg_ref[...])

        steps = [out_proj]
        for c in range(d_ff // chunk):
            steps += [up(c), down(c)]
        return steps, finish

    def largest_carry():
        top = carry_ref[0]
        for hp in range(1, n_pairs):
            top = jnp.maximum(top, carry_ref[hp])
        return jnp.max(top)

    def run_step(with_attention, with_mlp):
        slot = lax.rem(s, 2)
        if with_mlp:
            mlp_steps, mlp_finish = mlp_phases(1 - slot)
        else:
            mlp_steps, mlp_finish = [], None
        def store_block(hp, acc):
            d_ref[slot, :, cols[hp]] = jnp.where(
                first_head, acc[:blk], acc[blk:]).astype(d_ref.dtype)

        if with_attention:
            i = lax.rem(s, T // blk)
            q0 = pl.multiple_of(i * blk, blk)
            qs = []
            for hp in pairs:
                q2 = q_ref[:, cols[hp]]
                zero = jnp.zeros_like(q2)
                qs.append(jnp.concatenate(
                    [jnp.where(first_head, q2, zero), jnp.where(first_head, zero, q2)], axis=0))
            prev0 = pl.multiple_of(jnp.maximum(q0 - blk, 0), blk)
            prev_bias = jnp.where(i == 0, CHUNK_OFF, 0.0).astype(F32)
            prev_bias = jnp.broadcast_to(prev_bias, (2 * blk, V7X_LANES))
            groups = [list(pairs)[:n_pairs // 2], list(pairs)[n_pairs // 2:]]
            diag_sets = [attention_phases(qs, [(q0, causal, None)], lambda: [None] * n_pairs, g)
                         for g in groups]

            def carry_after_diag(state, group):
                out = [None] * n_pairs
                for idx, hp in enumerate(group):
                    out[hp] = -state["row_sum"][idx]
                return out

            prev_sets = [attention_phases(qs, [(prev0, None, prev_bias)],
                                          lambda st=ds[3], g=g: carry_after_diag(st, g), g)
                         for ds, g in zip(diag_sets, groups)]
            mlp_steps = list(mlp_steps)

            def mlp(count):
                for _ in range(min(count, len(mlp_steps))):
                    mlp_steps.pop(0)()

            for scores, _, _, _ in diag_sets + prev_sets:
                scores()
            mlp(2)
            for _, suffixes, _, _ in diag_sets + prev_sets:
                suffixes()
                mlp(1)
            mlp(len(mlp_steps))
            diag_out = [values() for _, _, values, _ in diag_sets]
            top = None
            for d_out, (_, _, values, _), group in zip(diag_out, prev_sets, groups):
                for (d_acc, _), (p_acc, carry), hp in zip(d_out, values(), group):
                    acc = d_acc + p_acc
                    acc_ref[hp] = acc
                    carry_ref[hp] = carry
                    store_block(hp, acc)
                    top = carry if top is None else jnp.maximum(top, carry)
            top = jnp.max(top)
        else:
            for step in mlp_steps:
                step()
        if with_mlp:
            mlp_finish()
        if with_attention:
            def more_to_do(state):
                t, top = state
                return jnp.logical_and(t < i - 1, top > WEIGHT_UNDERFLOW_LOG)

            def farther(state):
                t, _ = state
                k0 = pl.multiple_of((i - 2 - t) * blk, blk)
                sc, su, va, _ = attention_phases(qs, [(k0, None, None)],
                                                 lambda: [carry_ref[hp] for hp in pairs])
                sc()
                su()
                for hp, (pv, carry) in enumerate(va()):
                    acc_ref[hp] += pv
                    carry_ref[hp] = carry
                return t + 1, largest_carry()

            walked, _ = lax.while_loop(more_to_do, farther, (jnp.int32(0), top))

            @pl.when(walked > 0)
            def _():
                for hp in pairs:
                    store_block(hp, acc_ref[hp])

    @pl.when(s == 0)
    def _():
        wout_bf16[...] = wout_ref[...].astype(BF16)
        run_step(True, False)

    @pl.when(jnp.logical_and(s > 0, s < n_blocks))
    def _():
        run_step(True, True)

    @pl.when(s == n_blocks)
    def _():
        run_step(False, True)


def _attn_mlp_final(q, k, v, h, c, w_out, g, w1_layers, w2_layers, layer, final_g):
    B, T, W = q.shape
    N, D = h.shape
    blk = ATT_BLOCK
    assert T % blk == 0 and N == B * T and W % V7X_LANES == 0, (B, T, W, N)
    per_seq = T // blk
    n_blocks = N // blk
    neg_upper = -jnp.triu(jnp.ones((blk, blk), BF16), 1).T
    n_pairs = W // V7X_LANES

    def attn_block(s):
        return jnp.minimum(s, n_blocks - 1)

    def mlp_block(s):
        return jnp.maximum(s - 1, 0)

    q_spec = pl.BlockSpec((None, blk, W),
                          lambda s: (attn_block(s) // per_seq, attn_block(s) % per_seq, 0))
    kv_spec = pl.BlockSpec((None, T, W), lambda s: (attn_block(s) // per_seq, 0, 0))
    row_spec = pl.BlockSpec((blk, D), lambda s: (mlp_block(s), 0))
    half_spec = pl.BlockSpec((blk, W), lambda s: (mlp_block(s), 0))
    return pl.pallas_call(
        _attn_mlp_kernel,
        out_shape=jax.ShapeDtypeStruct((N, D), F32),
        grid=(n_blocks + 1,),
        in_specs=[q_spec, kv_spec, kv_spec, _resident(neg_upper.shape), row_spec, half_spec,
                  _resident(w_out.shape), _resident(g.shape),
                  _resident_layer(w1_layers.shape, layer),
                  _resident_layer(w2_layers.shape, layer), _resident(final_g.shape)],
        out_specs=row_spec,
        scratch_shapes=[pltpu.VMEM((n_pairs, 2 * blk, V7X_LANES), F32),
                        pltpu.VMEM((n_pairs, 2 * blk, V7X_LANES), F32),
                        pltpu.VMEM((2, blk, W), BF16),
                        pltpu.VMEM(w_out.shape, BF16)],
        compiler_params=pltpu.CompilerParams(
            dimension_semantics=("arbitrary",), vmem_limit_bytes=VMEM_LIMIT_BYTES),
        name="attention_out_mlp1_final",
    )(q, k, v, neg_upper, h, c, w_out, g, w1_layers, w2_layers, final_g)


def kernel(x, mix_norm_g, mlp_norm_g, ab_w_in, pool_w, pool_scale, conv_w, conv_b, ab_w_out,
           cd_w_in, sgu_norm_g, sgu_norm_b, sgu_w, sgu_b, cd_w_out, mlp_w1, mlp_w2, final_norm_g):
    B, T, D = x.shape
    N = B * T
    row = lambda vec: vec.reshape(1, -1)

    w1_layers, w2_layers = mlp_w1.astype(BF16), mlp_w2.astype(BF16)
    h = _layer0(x, row(mix_norm_g[0]), ab_w_in[0], pool_w[0], pool_scale[0], conv_w[0],
                row(conv_b[0]), ab_w_out[0], row(mlp_norm_g[0]), w1_layers, w2_layers, 0)

    b_s = jnp.broadcast_to(sgu_b[0][:, :, None], sgu_b[0].shape + (GROUP_WIDTH,))
    c, q, k, v = _l1_front(h, row(mix_norm_g[1]), cd_w_in[0], row(sgu_norm_g[0]),
                           row(sgu_norm_b[0]), sgu_w[0], b_s)
    shape3 = (B, T, MIXER_WIDTH)
    out = _attn_mlp_final(q.reshape(shape3), k.reshape(shape3), v.reshape(shape3), h, c,
                          cd_w_out[0], row(mlp_norm_g[1]), w1_layers, w2_layers, 1,
                          row(final_norm_g))
    return out.reshape(B, T, D)
```
